```python
import math
import jax, jax.numpy as jnp
from jax import lax
import numpy as np

D_MODEL = 1024
BATCH = 16
SEQ = 2048
DEPTH = 1

N_ATTN_HEADS = 8
HEAD_DIM = 64
ATTN_WIDTH = N_ATTN_HEADS * HEAD_DIM
N_CONV_GROUPS = 8
CONV_WIDTH = D_MODEL // 2
CONV_K = 3
D_FF = 2816
Q_BLOCK = 128
RMS_EPS = 1e-6
FFN_RESIDUAL_WEIGHT = 0.5
FORGET_BIAS_MEAN = 3.0

IN_SPLITS = (
    ATTN_WIDTH,
    ATTN_WIDTH,
    ATTN_WIDTH,
    N_ATTN_HEADS,
    CONV_WIDTH,
    CONV_WIDTH,
    CONV_WIDTH,
    D_MODEL,
    D_MODEL,
)
IN_COLS = sum(IN_SPLITS)

kernel_name = "fox_shortconv_gated_macaron_layer"


def rms_norm(x, g):
    xf = x.astype(jnp.float32)
    inv = lax.rsqrt(jnp.mean(xf * xf, axis=-1, keepdims=True) + RMS_EPS)
    return (xf * inv).astype(x.dtype) * g


def swiglu(x, w_gate, w_up, w_down):
    return (jax.nn.silu(x @ w_gate) * (x @ w_up)) @ w_down


def forgetting_attention(q, k, v, f_logits, b_forget):
    seq = q.shape[1]
    scale = 1.0 / math.sqrt(HEAD_DIM)
    log_f = jax.nn.log_sigmoid(f_logits.astype(jnp.float32) + b_forget.astype(jnp.float32))
    cum = jnp.transpose(jnp.cumsum(log_f, axis=1), (0, 2, 1))
    outs = []
    n_blocks = seq // Q_BLOCK
    for i in range(n_blocks):
        q0, q1 = i * Q_BLOCK, (i + 1) * Q_BLOCK
        kv_len = q1
        q_blk = q[:, q0:q1]
        k_pre = k[:, :kv_len]
        v_pre = v[:, :kv_len]
        s = jnp.einsum('bqhd,bkhd->bhqk', q_blk, k_pre).astype(jnp.float32) * scale
        s = s + cum[:, :, q0:q1, None] - cum[:, :, None, :kv_len]
        q_pos = jnp.arange(q0, q1)[:, None]
        k_pos = jnp.arange(kv_len)[None, :]
        s = jnp.where(q_pos >= k_pos, s, -jnp.inf)
        p = jax.nn.softmax(s, axis=-1).astype(v.dtype)
        outs.append(jnp.einsum('bhqk,bkhd->bqhd', p, v_pre))
    return jnp.concatenate(outs, axis=1)


def short_conv_mixer(xin, gate_b, gate_c, conv_w):
    seq = xin.shape[1]
    u = gate_c * xin
    up = jnp.pad(u, ((0, 0), (CONV_K - 1, 0), (0, 0)))
    conv = (conv_w[0] * up[:, 0:seq] + conv_w[1] * up[:, 1:seq + 1]
            + conv_w[2] * up[:, 2:seq + 2])
    return gate_b * conv


def setup_inputs(seed: int = 0) -> dict:
    key = jax.random.key(seed)
    ks = jax.random.split(key, 20)
    f32 = jnp.float32

    def lin(k, fan_in, fan_out):
        return jax.random.normal(k, (fan_in, fan_out), f32) * fan_in ** -0.5

    def gain(k, n):
        return jnp.ones((n,), f32) + 0.02 * jax.random.normal(k, (n,), f32)

    return {
        "x": jax.random.normal(ks[0], (BATCH, SEQ, D_MODEL), f32),
        "ffn1_norm": gain(ks[1], D_MODEL),
        "ffn1_gate": lin(ks[2], D_MODEL, D_FF),
        "ffn1_up": lin(ks[3], D_MODEL, D_FF),
        "ffn1_down": lin(ks[4], D_FF, D_MODEL),
        "mix_norm": gain(ks[5], D_MODEL),
        "w_in": lin(ks[6], D_MODEL, IN_COLS),
        "b_forget": FORGET_BIAS_MEAN + 0.5 * jax.random.normal(ks[7], (N_ATTN_HEADS,), f32),
        "conv_w": 0.5 * jax.random.normal(ks[8], (CONV_K, CONV_WIDTH), f32),
        "w_o_attn": lin(ks[9], ATTN_WIDTH, D_MODEL),
        "w_o_conv": lin(ks[10], CONV_WIDTH, D_MODEL),
        "w_out": lin(ks[11], D_MODEL, D_MODEL),
        "ffn2_norm": gain(ks[12], D_MODEL),
        "ffn2_gate": lin(ks[13], D_MODEL, D_FF),
        "ffn2_up": lin(ks[14], D_MODEL, D_FF),
        "ffn2_down": lin(ks[15], D_FF, D_MODEL),
        "final_norm": gain(ks[16], D_MODEL),
    }


def reference(x, ffn1_norm, ffn1_gate, ffn1_up, ffn1_down, mix_norm, w_in,
              b_forget, conv_w, w_o_attn, w_o_conv, w_out, ffn2_norm,
              ffn2_gate, ffn2_up, ffn2_down, final_norm):
    bsz, seq, _ = x.shape
    for _layer in range(DEPTH):
        x = x + FFN_RESIDUAL_WEIGHT * swiglu(rms_norm(x, ffn1_norm), ffn1_gate, ffn1_up, ffn1_down)

        h = rms_norm(x, mix_norm)
        proj = h @ w_in
        offsets = list(np.cumsum(IN_SPLITS)[:-1])
        q, k, v, f_log, c_b, c_c, c_x, g_attn, g_conv = jnp.split(proj, offsets, axis=-1)

        heads = lambda t: t.reshape(bsz, seq, N_ATTN_HEADS, HEAD_DIM)
        y_attn = forgetting_attention(heads(q), heads(k), heads(v), f_log, b_forget)
        y_attn = y_attn.reshape(bsz, seq, ATTN_WIDTH) @ w_o_attn

        y_conv = short_conv_mixer(c_x, c_b, c_c, conv_w) @ w_o_conv

        merged = jax.nn.sigmoid(g_attn) * y_attn + jax.nn.sigmoid(g_conv) * y_conv
        x = x + merged @ w_out

        x = x + FFN_RESIDUAL_WEIGHT * swiglu(rms_norm(x, ffn2_norm), ffn2_gate, ffn2_up, ffn2_down)
    return rms_norm(x, final_norm)
```

```python
import functools
import math

import jax
import jax.numpy as jnp
from jax import lax
from jax.experimental import pallas as pl
from jax.experimental.pallas import tpu as pltpu

F32 = jnp.float32
BF16 = jnp.bfloat16

D_MODEL = 1024
N_HEADS = 8
HEAD_DIM = 64
ATTN_W = N_HEADS * HEAD_DIM
CONV_W = D_MODEL // 2
D_FF = 2816
RMS_EPS = 1e-6
FFN_RES = 0.5
LANES = 128
QKVF_COLS = 3 * ATTN_W + LANES
GATE_COLS = 3 * CONV_W + 2 * D_MODEL
FF_CHUNK = 256
TM_A = 512
TM_C = 256
TQ = 256
NEG_BIG = -1e30
VMEM_LIMIT = 56 * 1024 * 1024


def _rms(x, g):
    inv = lax.rsqrt(jnp.mean(x * x, axis=-1, keepdims=True) + RMS_EPS)
    return (x * inv) * g


def _swiglu(h_scr, act_scr, wg_ref, wu_ref, wd_ref):
    for c in range(D_FF // FF_CHUNK):
        sl = slice(c * FF_CHUNK, (c + 1) * FF_CHUNK)
        g = jnp.dot(h_scr[...], wg_ref[:, sl], preferred_element_type=F32)
        u = jnp.dot(h_scr[...], wu_ref[:, sl], preferred_element_type=F32)
        act_scr[:, sl] = (g * jax.nn.sigmoid(g) * u).astype(BF16)
    return jnp.dot(act_scr[...], wd_ref[...], preferred_element_type=F32)


def _cumsum_rows(x):
    n = x.shape[0]
    row = lax.broadcasted_iota(jnp.int32, x.shape, 0)
    d = 1
    while d < n:
        x = x + jnp.where(row >= d, pltpu.roll(x, d, axis=0), 0.0)
        d *= 2
    return x


def _ffn1_qkv_body(x_ref, g1_ref, wg_ref, wu_ref, wd_ref, g2_ref, wa_ref, bf_ref,
                   x1_ref, qk_ref, vt_ref, cc_ref, ct_ref,
                   h_scr, act_scr, carry_scr):
    x = x_ref[...]
    h_scr[...] = _rms(x, g1_ref[...]).astype(BF16)
    x1 = x + FFN_RES * _swiglu(h_scr, act_scr, wg_ref, wu_ref, wd_ref)
    x1_ref[...] = x1

    h2 = _rms(x1, g2_ref[...]).astype(BF16)
    pr = jnp.dot(h2, wa_ref[...], preferred_element_type=F32)
    qk_ref[:, :ATTN_W] = (pr[:, :ATTN_W] * (1.0 / math.sqrt(HEAD_DIM))).astype(BF16)
    qk_ref[:, ATTN_W:] = pr[:, ATTN_W:2 * ATTN_W].astype(BF16)
    vt_ref[...] = pr[:, 2 * ATTN_W:3 * ATTN_W].T.astype(BF16)

    @pl.when(pl.program_id(1) == 0)
    def _():
        carry_scr[...] = jnp.zeros_like(carry_scr)

    z = pr[:, 3 * ATTN_W:] + bf_ref[...]
    log_f = jnp.minimum(z, 0.0) - jnp.log1p(jnp.exp(-jnp.abs(z)))
    cum = _cumsum_rows(log_f) + carry_scr[...]
    carry_scr[...] = cum[TM_A - 1:TM_A, :]
    cc_ref[...] = cum
    ct_ref[...] = cum.T[:N_HEADS, :]


def _fox_attn_body(q_ref, k_ref, vt_ref, cc_ref, ct_ref, y_ref, ot_scr):
    qi = pl.program_id(1)
    q0 = pl.multiple_of(qi * TQ, TQ)
    lane = lax.broadcasted_iota(jnp.int32, (TQ, LANES), 1)
    k_idx = lax.broadcasted_iota(jnp.int32, (TQ, TQ), 0)
    q_idx = lax.broadcasted_iota(jnp.int32, (TQ, TQ), 1)
    causal = q_idx >= k_idx

    def block(pair, k0, masked, state):
        qp = q_ref[:, pair * LANES:(pair + 1) * LANES]
        kp = k_ref[pl.ds(k0, TQ), pair * LANES:(pair + 1) * LANES]
        new_state = []
        for hh in range(2):
            h = 2 * pair + hh
            m, l, ot = state[hh]
            in_head = (lane >= hh * HEAD_DIM) & (lane < (hh + 1) * HEAD_DIM)
            kh = jnp.where(in_head, kp, jnp.zeros_like(kp))
            st = lax.dot_general(kh, qp, (((1,), (1,)), ((), ())),
                                 preferred_element_type=F32)
            cq = ct_ref[h:h + 1, pl.ds(q0, TQ)]
            ck = cc_ref[pl.ds(k0, TQ), h:h + 1]
            st = st + cq - ck
            if masked:
                st = jnp.where(causal, st, NEG_BIG)
            m_new = jnp.maximum(m, jnp.max(st, axis=0, keepdims=True))
            alpha = jnp.exp(m - m_new)
            p = jnp.exp(st - m_new)
            l = alpha * l + jnp.sum(p, axis=0, keepdims=True)
            vt = vt_ref[h * HEAD_DIM:(h + 1) * HEAD_DIM, pl.ds(k0, TQ)]
            ot = alpha * ot + jnp.dot(vt, p.astype(BF16), preferred_element_type=F32)
            new_state.append((m_new, l, ot))
        return tuple(new_state)

    for pair in range(N_HEADS // 2):
        init = tuple((jnp.full((1, TQ), NEG_BIG, F32), jnp.zeros((1, TQ), F32),
                      jnp.zeros((HEAD_DIM, TQ), F32)) for _ in range(2))
        state = lax.fori_loop(
            0, qi,
            lambda j, s: block(pair, pl.multiple_of(j * TQ, TQ), False, s),
            init)
        state = block(pair, q0, True, state)
        for hh in range(2):
            h = 2 * pair + hh
            _, l, ot = state[hh]
            ot_scr[h * HEAD_DIM:(h + 1) * HEAD_DIM, :] = ot / l
    y_ref[...] = ot_scr[...].T.astype(BF16)


def _mix_ffn2_body(x1_ref, y_ref, gm_ref, wgate_ref, cw_ref, woa_ref, woc_ref, wout_ref,
                   g3_ref, wg_ref, wu_ref, wd_ref, gf_ref,
                   o_ref, h_scr, act_scr, tail_scr):
    x1 = x1_ref[...]
    h_scr[...] = _rms(x1, gm_ref[...]).astype(BF16)

    def gate(lo, hi):
        return jnp.dot(h_scr[...], wgate_ref[:, lo:hi], preferred_element_type=F32)

    c_b = gate(0, CONV_W)
    u = gate(CONV_W, 2 * CONV_W) * gate(2 * CONV_W, 3 * CONV_W)

    @pl.when(pl.program_id(1) == 0)
    def _():
        tail_scr[...] = jnp.zeros_like(tail_scr)

    row = lax.broadcasted_iota(jnp.int32, u.shape, 0)
    prev1 = tail_scr[1:2, :]
    prev2 = tail_scr[0:1, :]
    u1 = jnp.where(row == 0, prev1, pltpu.roll(u, 1, axis=0))
    u2 = jnp.where(row == 0, prev2, jnp.where(row == 1, prev1, pltpu.roll(u, 2, axis=0)))
    tail_scr[...] = u[TM_C - 2:TM_C, :]
    conv = cw_ref[0:1, :] * u2 + cw_ref[1:2, :] * u1 + cw_ref[2:3, :] * u
    y_conv = jnp.dot((c_b * conv).astype(BF16), woc_ref[...], preferred_element_type=F32)
    y_attn = jnp.dot(y_ref[...], woa_ref[...], preferred_element_type=F32)

    o0 = 3 * CONV_W
    merged = (jax.nn.sigmoid(gate(o0, o0 + D_MODEL)) * y_attn
              + jax.nn.sigmoid(gate(o0 + D_MODEL, o0 + 2 * D_MODEL)) * y_conv)
    x2 = x1 + jnp.dot(merged.astype(BF16), wout_ref[...], preferred_element_type=F32)

    h_scr[...] = _rms(x2, g3_ref[...]).astype(BF16)
    x3 = x2 + FFN_RES * _swiglu(h_scr, act_scr, wg_ref, wu_ref, wd_ref)
    o_ref[...] = _rms(x3, gf_ref[...])


def _resident(shape):
    return pl.BlockSpec(shape, lambda b, i: (0,) * len(shape), pipeline_mode=pl.Buffered(1))


def kernel(x, ffn1_norm, ffn1_gate, ffn1_up, ffn1_down, mix_norm, w_in, b_forget, conv_w,
           w_o_attn, w_o_conv, w_out, ffn2_norm, ffn2_gate, ffn2_up, ffn2_down, final_norm):
    bsz, seq, d = x.shape
    assert d == D_MODEL and seq % TM_A == 0 and seq % TM_C == 0 and seq % TQ == 0
    row = lambda v: v.reshape(1, -1).astype(F32)
    n_qkvf = 3 * ATTN_W + N_HEADS
    w_a = jnp.pad(w_in[:, :n_qkvf], ((0, 0), (0, QKVF_COLS - n_qkvf))).astype(BF16)
    w_gate = w_in[:, n_qkvf:].astype(BF16)
    b_f = jnp.pad(row(b_forget), ((0, 0), (0, LANES - N_HEADS)))
    params = pltpu.CompilerParams(dimension_semantics=("arbitrary", "arbitrary"),
                                  vmem_limit_bytes=VMEM_LIMIT)

    tile_a = lambda w: pl.BlockSpec((None, TM_A, w), lambda b, i: (b, i, 0))
    x1, qk, vt, cum_c, cum_t = pl.pallas_call(
        _ffn1_qkv_body,
        grid=(bsz, seq // TM_A),
        in_specs=[tile_a(D_MODEL), _resident((1, D_MODEL)),
                  _resident((D_MODEL, D_FF)), _resident((D_MODEL, D_FF)),
                  _resident((D_FF, D_MODEL)), _resident((1, D_MODEL)),
                  _resident((D_MODEL, QKVF_COLS)), _resident((1, LANES))],
        out_specs=[tile_a(D_MODEL), tile_a(2 * ATTN_W),
                   pl.BlockSpec((None, ATTN_W, TM_A), lambda b, i: (b, 0, i)),
                   tile_a(LANES),
                   pl.BlockSpec((None, N_HEADS, TM_A), lambda b, i: (b, 0, i))],
        out_shape=[jax.ShapeDtypeStruct((bsz, seq, D_MODEL), F32),
                   jax.ShapeDtypeStruct((bsz, seq, 2 * ATTN_W), BF16),
                   jax.ShapeDtypeStruct((bsz, ATTN_W, seq), BF16),
                   jax.ShapeDtypeStruct((bsz, seq, LANES), F32),
                   jax.ShapeDtypeStruct((bsz, N_HEADS, seq), F32)],
        scratch_shapes=[pltpu.VMEM((TM_A, D_MODEL), BF16), pltpu.VMEM((TM_A, D_FF), BF16),
                        pltpu.VMEM((1, LANES), F32)],
        compiler_params=params,
        name="ffn1_qkv",
    )(x, row(ffn1_norm), ffn1_gate.astype(BF16), ffn1_up.astype(BF16),
      ffn1_down.astype(BF16), row(mix_norm), w_a, b_f)

    y = pl.pallas_call(
        _fox_attn_body,
        grid=(bsz, seq // TQ),
        in_specs=[pl.BlockSpec((None, TQ, ATTN_W), lambda b, i: (b, i, 0)),
                  pl.BlockSpec((None, seq, ATTN_W), lambda b, i: (b, 0, 1)),
                  pl.BlockSpec((None, ATTN_W, seq), lambda b, i: (b, 0, 0)),
                  pl.BlockSpec((None, seq, LANES), lambda b, i: (b, 0, 0)),
                  pl.BlockSpec((None, N_HEADS, seq), lambda b, i: (b, 0, 0))],
        out_specs=pl.BlockSpec((None, TQ, ATTN_W), lambda b, i: (b, i, 0)),
        out_shape=jax.ShapeDtypeStruct((bsz, seq, ATTN_W), BF16),
        scratch_shapes=[pltpu.VMEM((ATTN_W, TQ), F32)],
        compiler_params=params,
        name="fox_attn",
    )(qk, qk, vt, cum_c, cum_t)

    tile_c = lambda w: pl.BlockSpec((None, TM_C, w), lambda b, i: (b, i, 0))
    return pl.pallas_call(
        _mix_ffn2_body,
        grid=(bsz, seq // TM_C),
        in_specs=[tile_c(D_MODEL), tile_c(ATTN_W), _resident((1, D_MODEL)),
                  _resident((D_MODEL, GATE_COLS)), _resident((3, CONV_W)),
                  _resident((ATTN_W, D_MODEL)), _resident((CONV_W, D_MODEL)),
                  _resident((D_MODEL, D_MODEL)), _resident((1, D_MODEL)),
                  _resident((D_MODEL, D_FF)), _resident((D_MODEL, D_FF)),
                  _resident((D_FF, D_MODEL)), _resident((1, D_MODEL))],
        out_specs=tile_c(D_MODEL),
        out_shape=jax.ShapeDtypeStruct((bsz, seq, D_MODEL), F32),
        scratch_shapes=[pltpu.VMEM((TM_C, D_MODEL), BF16), pltpu.VMEM((TM_C, D_FF), BF16),
                        pltpu.VMEM((2, CONV_W), F32)],
        compiler_params=params,
        name="mix_ffn2",
    )(x1, y, row(mix_norm), w_gate, conv_w.astype(F32), w_o_attn.astype(BF16),
      w_o_conv.astype(BF16), w_out.astype(BF16), row(ffn2_norm), ffn2_gate.astype(BF16),
      ffn2_up.astype(BF16), ffn2_down.astype(BF16), row(final_norm))
```

```python
import math

import numpy as np
import jax
import jax.numpy as jnp
from jax import lax
from jax.experimental import pallas as pl
from jax.experimental.pallas import tpu as pltpu

F32 = jnp.float32
BF16 = jnp.bfloat16

D_MODEL = 1024
N_HEADS = 8
HEAD_DIM = 64
ATTN_W = N_HEADS * HEAD_DIM
CONV_W = D_MODEL // 2
D_FF = 2816
RMS_EPS = 1e-6
FFN_RES = 0.5
LANES = 128
QKVF_COLS = 3 * ATTN_W + LANES
GATE_COLS = 3 * CONV_W + 2 * D_MODEL
FF_CHUNK = 256
TM_A = 512
TM_C = 256
TQ = 256
TK = 256
NEG_BIG = -1e30
VT_ROWS = HEAD_DIM + 16
N_SPLIT = 3
BIAS_SLOT = 8
ONES_LANE = N_HEADS
VMEM_LIMIT = 56 * 1024 * 1024


def _rms(x, g):
    inv = lax.rsqrt(jnp.mean(x * x, axis=-1, keepdims=True) + RMS_EPS)
    return (x * inv) * g


def _swiglu(h_scr, act_scr, wg_ref, wu_ref, wd_ref):
    for c in range(D_FF // FF_CHUNK):
        sl = slice(c * FF_CHUNK, (c + 1) * FF_CHUNK)
        g = jnp.dot(h_scr[...], wg_ref[:, sl], preferred_element_type=F32)
        u = jnp.dot(h_scr[...], wu_ref[:, sl], preferred_element_type=F32)
        act_scr[:, sl] = (g * jax.nn.sigmoid(g) * u).astype(BF16)
    return jnp.dot(act_scr[...], wd_ref[...], preferred_element_type=F32)


def _cumsum_rows(x):
    n = x.shape[0]
    row = lax.broadcasted_iota(jnp.int32, x.shape, 0)
    d = 1
    while d < n:
        x = x + jnp.where(row >= d, pltpu.roll(x, d, axis=0), 0.0)
        d *= 2
    return x


def _ffn1_qkv_body(x_ref, g1_ref, wg_ref, wu_ref, wd_ref, g2_ref, wa_ref, bf_ref,
                   x1_ref, qk_ref, vt_ref, cc_ref,
                   h_scr, act_scr, carry_scr):
    x = x_ref[...]
    h_scr[...] = _rms(x, g1_ref[...]).astype(BF16)
    x1 = x + FFN_RES * _swiglu(h_scr, act_scr, wg_ref, wu_ref, wd_ref)
    x1_ref[...] = x1

    h2 = _rms(x1, g2_ref[...]).astype(BF16)
    pr = jnp.dot(h2, wa_ref[...], preferred_element_type=F32)
    qk_ref[:, :ATTN_W] = (pr[:, :ATTN_W] * (1.0 / math.sqrt(HEAD_DIM))).astype(BF16)
    qk_ref[:, ATTN_W:] = pr[:, ATTN_W:2 * ATTN_W].astype(BF16)
    v_t = pr[:, 2 * ATTN_W:3 * ATTN_W].T.astype(BF16)
    for h in range(N_HEADS):
        vt_ref[h * VT_ROWS:h * VT_ROWS + HEAD_DIM, :] = v_t[h * HEAD_DIM:(h + 1) * HEAD_DIM, :]
        vt_ref[h * VT_ROWS + HEAD_DIM:(h + 1) * VT_ROWS, :] = jnp.ones(
            (VT_ROWS - HEAD_DIM, TM_A), BF16)

    @pl.when(pl.program_id(1) == 0)
    def _():
        carry_scr[...] = jnp.zeros_like(carry_scr)

    z = pr[:, 3 * ATTN_W:] + bf_ref[...]
    log_f = jnp.minimum(z, 0.0) - jnp.log1p(jnp.exp(-jnp.abs(z)))
    cum = _cumsum_rows(log_f) + carry_scr[...]
    carry_scr[...] = cum[TM_A - 1:TM_A, :]
    cc_ref[...] = cum


def _split_cum(c):
    pieces = []
    r = c
    for _ in range(N_SPLIT):
        p = r.astype(BF16)
        pieces.append(p)
        r = r - p.astype(F32)
    lane = lax.broadcasted_iota(jnp.int32, c.shape, 1)
    pieces[0] = jnp.where(lane == ONES_LANE, jnp.ones_like(pieces[0]), pieces[0])
    return jnp.concatenate(pieces, axis=1)


def _bias_selectors():
    sq = np.zeros((N_SPLIT * LANES, ATTN_W), np.float32)
    sk = np.zeros((N_SPLIT * LANES, ATTN_W), np.float32)
    for h in range(N_HEADS):
        base = (h // 2) * LANES + (h % 2) * BIAS_SLOT
        for s in range(N_SPLIT):
            sq[s * LANES + h, base + s] = 1.0
            sq[ONES_LANE, base + N_SPLIT + s] = 1.0
            sk[ONES_LANE, base + s] = 1.0
            sk[s * LANES + h, base + N_SPLIT + s] = -1.0
    return jnp.asarray(sq, BF16), jnp.asarray(sk, BF16)


def _fox_attn_body(q_ref, k_ref, vt_ref, cc_ref, sq_ref, sk_ref, y_ref,
                   kf_scr, qc_scr, m_scr, ot_scr, yt_scr):
    qi = pl.program_id(1)
    seq = k_ref.shape[0]
    q0 = pl.multiple_of(qi * TQ, TQ)
    lane = lax.broadcasted_iota(jnp.int32, (TK, LANES), 1)

    @pl.when(qi == 0)
    def _():
        def chunk(r, carry):
            r0 = pl.multiple_of(r * TK, TK)
            ka = jnp.dot(_split_cum(cc_ref[pl.ds(r0, TK), :]), sk_ref[...],
                         preferred_element_type=F32).astype(BF16)
            for pair in range(N_HEADS // 2):
                cols = slice(pair * LANES, (pair + 1) * LANES)
                kp = k_ref[pl.ds(r0, TK), cols]
                kap = ka[:, cols]
                for hh in range(2):
                    own = (lane >= hh * HEAD_DIM) & (lane < (hh + 1) * HEAD_DIM)
                    own_b = (lane >= hh * BIAS_SLOT) & (lane < (hh + 1) * BIAS_SLOT)
                    rows = slice(hh * TK, (hh + 1) * TK)
                    kf_scr[pair, r, rows, :LANES] = jnp.where(own, kp, jnp.zeros_like(kp))
                    kf_scr[pair, r, rows, LANES:] = jnp.where(own_b, kap, jnp.zeros_like(kap))
            return carry
        lax.fori_loop(0, seq // TK, chunk, 0)

    qa = jnp.dot(_split_cum(cc_ref[pl.ds(q0, TQ), :]), sq_ref[...],
                 preferred_element_type=F32).astype(BF16)
    for pair in range(N_HEADS // 2):
        cols = slice(pair * LANES, (pair + 1) * LANES)
        qc_scr[pair, :, :LANES] = q_ref[:, cols]
        qc_scr[pair, :, LANES:] = qa[:, cols]
    m_scr[...] = jnp.full(m_scr.shape, NEG_BIG, F32)
    ot_scr[...] = jnp.zeros(ot_scr.shape, F32)

    k_idx = lax.broadcasted_iota(jnp.int32, (TK, TQ), 0)
    q_idx = lax.broadcasted_iota(jnp.int32, (TK, TQ), 1)

    def block(j, diag_offset):
        k0 = pl.multiple_of(j * TK, TK)

        def scores(pair):
            return lax.dot_general(kf_scr[pair, j], qc_scr[pair], (((1,), (1,)), ((), ())),
                                   preferred_element_type=F32)

        def head(h, st_pair):
            st = st_pair[(h % 2) * TK:(h % 2 + 1) * TK, :]
            if diag_offset is not None:
                st = jnp.where(q_idx >= k_idx + diag_offset, st, NEG_BIG)
            m_old = m_scr[h:h + 1, :]
            m_new = jnp.maximum(m_old, jnp.max(st, axis=0, keepdims=True))
            alpha = jnp.exp(m_old - m_new)
            p = jnp.exp(st - m_new).astype(BF16)
            m_scr[h:h + 1, :] = m_new
            rows = slice(h * VT_ROWS, (h + 1) * VT_ROWS)
            ot_scr[rows, :] = alpha * ot_scr[rows, :] + jnp.dot(
                vt_ref[rows, pl.ds(k0, TK)], p, preferred_element_type=F32)

        n_pairs = N_HEADS // 2
        st_next = scores(0)
        for pair in range(n_pairs):
            st_cur = st_next
            if pair + 1 < n_pairs:
                st_next = scores(pair + 1)
            head(2 * pair, st_cur)
            head(2 * pair + 1, st_cur)

    def off_diag(j, carry):
        block(j, None)
        return carry
    lax.fori_loop(0, qi * (TQ // TK), off_diag, 0)
    for s in range(TQ // TK):
        block(qi * (TQ // TK) + s, s * TK)

    for h in range(N_HEADS):
        o = ot_scr[h * VT_ROWS:h * VT_ROWS + HEAD_DIM, :]
        l = ot_scr[h * VT_ROWS + HEAD_DIM:h * VT_ROWS + HEAD_DIM + 1, :]
        yt_scr[h * HEAD_DIM:(h + 1) * HEAD_DIM, :] = o / l
    y_ref[...] = yt_scr[...].T.astype(BF16)


def _mix_ffn2_body(x1_ref, y_ref, gm_ref, wgate_ref, cw_ref, woa_ref, woc_ref, wout_ref,
                   g3_ref, wg_ref, wu_ref, wd_ref, gf_ref,
                   o_ref, h_scr, act_scr, tail_scr):
    x1 = x1_ref[...]
    h_scr[...] = _rms(x1, gm_ref[...]).astype(BF16)

    def gate(lo, hi):
        return jnp.dot(h_scr[...], wgate_ref[:, lo:hi], preferred_element_type=F32)

    c_b = gate(0, CONV_W)
    u = gate(CONV_W, 2 * CONV_W) * gate(2 * CONV_W, 3 * CONV_W)

    @pl.when(pl.program_id(1) == 0)
    def _():
        tail_scr[...] = jnp.zeros_like(tail_scr)

    row = lax.broadcasted_iota(jnp.int32, u.shape, 0)
    prev1 = tail_scr[1:2, :]
    prev2 = tail_scr[0:1, :]
    u1 = jnp.where(row == 0, prev1, pltpu.roll(u, 1, axis=0))
    u2 = jnp.where(row == 0, prev2, jnp.where(row == 1, prev1, pltpu.roll(u, 2, axis=0)))
    tail_scr[...] = u[TM_C - 2:TM_C, :]
    conv = cw_ref[0:1, :] * u2 + cw_ref[1:2, :] * u1 + cw_ref[2:3, :] * u
    y_conv = jnp.dot((c_b * conv).astype(BF16), woc_ref[...], preferred_element_type=F32)
    y_attn = jnp.dot(y_ref[...], woa_ref[...], preferred_element_type=F32)

    o0 = 3 * CONV_W
    merged = (jax.nn.sigmoid(gate(o0, o0 + D_MODEL)) * y_attn
              + jax.nn.sigmoid(gate(o0 + D_MODEL, o0 + 2 * D_MODEL)) * y_conv)
    x2 = x1 + jnp.dot(merged.astype(BF16), wout_ref[...], preferred_element_type=F32)

    h_scr[...] = _rms(x2, g3_ref[...]).astype(BF16)
    x3 = x2 + FFN_RES * _swiglu(h_scr, act_scr, wg_ref, wu_ref, wd_ref)
    o_ref[...] = _rms(x3, gf_ref[...])


def _resident(shape):
    return pl.BlockSpec(shape, lambda b, i: (0,) * len(shape), pipeline_mode=pl.Buffered(1))


def kernel(x, ffn1_norm, ffn1_gate, ffn1_up, ffn1_down, mix_norm, w_in, b_forget, conv_w,
           w_o_attn, w_o_conv, w_out, ffn2_norm, ffn2_gate, ffn2_up, ffn2_down, final_norm):
    bsz, seq, d = x.shape
    assert d == D_MODEL and seq % TM_A == 0 and seq % TM_C == 0 and seq % TQ == 0
    row = lambda v: v.reshape(1, -1).astype(F32)
    n_qkvf = 3 * ATTN_W + N_HEADS
    w_a = jnp.pad(w_in[:, :n_qkvf], ((0, 0), (0, QKVF_COLS - n_qkvf))).astype(BF16)
    w_gate = w_in[:, n_qkvf:].astype(BF16)
    b_f = jnp.pad(row(b_forget), ((0, 0), (0, LANES - N_HEADS)))
    sel_q, sel_k = _bias_selectors()
    params = pltpu.CompilerParams(dimension_semantics=("arbitrary", "arbitrary"),
                                  vmem_limit_bytes=VMEM_LIMIT)

    tile_a = lambda w: pl.BlockSpec((None, TM_A, w), lambda b, i: (b, i, 0))
    x1, qk, vt, cum = pl.pallas_call(
        _ffn1_qkv_body,
        grid=(bsz, seq // TM_A),
        in_specs=[tile_a(D_MODEL), _resident((1, D_MODEL)),
                  _resident((D_MODEL, D_FF)), _resident((D_MODEL, D_FF)),
                  _resident((D_FF, D_MODEL)), _resident((1, D_MODEL)),
                  _resident((D_MODEL, QKVF_COLS)), _resident((1, LANES))],
        out_specs=[tile_a(D_MODEL), tile_a(2 * ATTN_W),
                   pl.BlockSpec((None, N_HEADS * VT_ROWS, TM_A), lambda b, i: (b, 0, i)),
                   tile_a(LANES)],
        out_shape=[jax.ShapeDtypeStruct((bsz, seq, D_MODEL), F32),
                   jax.ShapeDtypeStruct((bsz, seq, 2 * ATTN_W), BF16),
                   jax.ShapeDtypeStruct((bsz, N_HEADS * VT_ROWS, seq), BF16),
                   jax.ShapeDtypeStruct((bsz, seq, LANES), F32)],
        scratch_shapes=[pltpu.VMEM((TM_A, D_MODEL), BF16), pltpu.VMEM((TM_A, D_FF), BF16),
                        pltpu.VMEM((1, LANES), F32)],
        compiler_params=params,
        name="ffn1_qkv",
    )(x, row(ffn1_norm), ffn1_gate.astype(BF16), ffn1_up.astype(BF16),
      ffn1_down.astype(BF16), row(mix_norm), w_a, b_f)

    y = pl.pallas_call(
        _fox_attn_body,
        grid=(bsz, seq // TQ),
        in_specs=[pl.BlockSpec((None, TQ, ATTN_W), lambda b, i: (b, i, 0)),
                  pl.BlockSpec((None, seq, ATTN_W), lambda b, i: (b, 0, 1)),
                  pl.BlockSpec((None, N_HEADS * VT_ROWS, seq), lambda b, i: (b, 0, 0)),
                  pl.BlockSpec((None, seq, LANES), lambda b, i: (b, 0, 0)),
                  _resident((N_SPLIT * LANES, ATTN_W)), _resident((N_SPLIT * LANES, ATTN_W))],
        out_specs=pl.BlockSpec((None, TQ, ATTN_W), lambda b, i: (b, i, 0)),
        out_shape=jax.ShapeDtypeStruct((bsz, seq, ATTN_W), BF16),
        scratch_shapes=[pltpu.VMEM((N_HEADS // 2, seq // TK, 2 * TK, 2 * LANES), BF16),
                        pltpu.VMEM((N_HEADS // 2, TQ, 2 * LANES), BF16),
                        pltpu.VMEM((N_HEADS, TQ), F32),
                        pltpu.VMEM((N_HEADS * VT_ROWS, TQ), F32),
                        pltpu.VMEM((ATTN_W, TQ), F32)],
        compiler_params=params,
        name="fox_attn",
    )(qk, qk, vt, cum, sel_q, sel_k)

    tile_c = lambda w: pl.BlockSpec((None, TM_C, w), lambda b, i: (b, i, 0))
    return pl.pallas_call(
        _mix_ffn2_body,
        grid=(bsz, seq // TM_C),
        in_specs=[tile_c(D_MODEL), tile_c(ATTN_W), _resident((1, D_MODEL)),
                  _resident((D_MODEL, GATE_COLS)), _resident((3, CONV_W)),
                  _resident((ATTN_W, D_MODEL)), _resident((CONV_W, D_MODEL)),
                  _resident((D_MODEL, D_MODEL)), _resident((1, D_MODEL)),
                  _resident((D_MODEL, D_FF)), _resident((D_MODEL, D_FF)),
                  _resident((D_FF, D_MODEL)), _resident((1, D_MODEL))],
        out_specs=tile_c(D_MODEL),
        out_shape=jax.ShapeDtypeStruct((bsz, seq, D_MODEL), F32),
        scratch_shapes=[pltpu.VMEM((TM_C, D_MODEL), BF16), pltpu.VMEM((TM_C, D_FF), BF16),
                        pltpu.VMEM((2, CONV_W), F32)],
        compiler_params=params,
        name="mix_ffn2",
    )(x1, y, row(mix_norm), w_gate, conv_w.astype(F32), w_o_attn.astype(BF16),
      w_o_conv.astype(BF16), w_out.astype(BF16), row(ffn2_norm), ffn2_gate.astype(BF16),
      ffn2_up.astype(BF16), ffn2_down.astype(BF16), row(final_norm))
```

```python
import math

import numpy as np
import jax
import jax.numpy as jnp
from jax import lax
from jax.experimental import pallas as pl
from jax.experimental.pallas import tpu as pltpu

F32 = jnp.float32
BF16 = jnp.bfloat16

D_MODEL = 1024
N_HEADS = 8
HEAD_DIM = 64
ATTN_W = N_HEADS * HEAD_DIM
CONV_W = D_MODEL // 2
D_FF = 2816
RMS_EPS = 1e-6
FFN_RES = 0.5
LANES = 128
QKVF_COLS = 3 * ATTN_W + LANES
GATE_COLS = 3 * CONV_W + 2 * D_MODEL
FF_CHUNK = 256
TM_A = 512
TM_C = 256
TQ = 256
TK = TQ
SCORES_AHEAD = 3
NEG_BIG = -1e30
VT_ROWS = HEAD_DIM + 16
N_SPLIT = 3
BIAS_SLOT = 8
ONES_LANE = N_HEADS
VMEM_LIMIT = 56 * 1024 * 1024


def _rms(x, g):
    inv = lax.rsqrt(jnp.mean(x * x, axis=-1, keepdims=True) + RMS_EPS)
    return (x * inv) * g


def _swiglu(h_scr, act_scr, wg_ref, wu_ref, wd_ref):
    for c in range(D_FF // FF_CHUNK):
        sl = slice(c * FF_CHUNK, (c + 1) * FF_CHUNK)
        g = jnp.dot(h_scr[...], wg_ref[:, sl], preferred_element_type=F32)
        u = jnp.dot(h_scr[...], wu_ref[:, sl], preferred_element_type=F32)
        act_scr[:, sl] = (g * jax.nn.sigmoid(g) * u).astype(BF16)
    return jnp.dot(act_scr[...], wd_ref[...], preferred_element_type=F32)


def _cumsum_rows(x):
    n = x.shape[0]
    row = lax.broadcasted_iota(jnp.int32, x.shape, 0)
    d = 1
    while d < n:
        x = x + jnp.where(row >= d, pltpu.roll(x, d, axis=0), 0.0)
        d *= 2
    return x


def _ffn1_qkv_body(x_ref, g1_ref, wg_ref, wu_ref, wd_ref, g2_ref, wa_ref, bf_ref,
                   x1_ref, qk_ref, vt_ref, cc_ref,
                   h_scr, act_scr, carry_scr):
    x = x_ref[...]
    h_scr[...] = _rms(x, g1_ref[...]).astype(BF16)
    x1 = x + FFN_RES * _swiglu(h_scr, act_scr, wg_ref, wu_ref, wd_ref)
    x1_ref[...] = x1

    h2 = _rms(x1, g2_ref[...]).astype(BF16)
    pr = jnp.dot(h2, wa_ref[...], preferred_element_type=F32)
    qk_ref[:, :ATTN_W] = (pr[:, :ATTN_W] * (1.0 / math.sqrt(HEAD_DIM))).astype(BF16)
    qk_ref[:, ATTN_W:] = pr[:, ATTN_W:2 * ATTN_W].astype(BF16)
    v_t = pr[:, 2 * ATTN_W:3 * ATTN_W].T.astype(BF16)
    for h in range(N_HEADS):
        vt_ref[h * VT_ROWS:h * VT_ROWS + HEAD_DIM, :] = v_t[h * HEAD_DIM:(h + 1) * HEAD_DIM, :]
        vt_ref[h * VT_ROWS + HEAD_DIM:(h + 1) * VT_ROWS, :] = jnp.ones(
            (VT_ROWS - HEAD_DIM, TM_A), BF16)

    @pl.when(pl.program_id(1) == 0)
    def _():
        carry_scr[...] = jnp.zeros_like(carry_scr)

    z = pr[:, 3 * ATTN_W:] + bf_ref[...]
    log_f = jnp.minimum(z, 0.0) - jnp.log1p(jnp.exp(-jnp.abs(z)))
    cum = _cumsum_rows(log_f) + carry_scr[...]
    carry_scr[...] = cum[TM_A - 1:TM_A, :]
    cc_ref[...] = cum


def _split_cum(c):
    pieces = []
    r = c
    for _ in range(N_SPLIT):
        p = r.astype(BF16)
        pieces.append(p)
        r = r - p.astype(F32)
    lane = lax.broadcasted_iota(jnp.int32, c.shape, 1)
    pieces[0] = jnp.where(lane == ONES_LANE, jnp.ones_like(pieces[0]), pieces[0])
    return jnp.concatenate(pieces, axis=1)


def _bias_selectors():
    sq = np.zeros((N_SPLIT * LANES, ATTN_W), np.float32)
    sk = np.zeros((N_SPLIT * LANES, ATTN_W), np.float32)
    for h in range(N_HEADS):
        base = (h // 2) * LANES + (h % 2) * BIAS_SLOT
        for s in range(N_SPLIT):
            sq[s * LANES + h, base + s] = 1.0
            sq[ONES_LANE, base + N_SPLIT + s] = 1.0
            sk[ONES_LANE, base + s] = 1.0
            sk[s * LANES + h, base + N_SPLIT + s] = -1.0
    return jnp.asarray(sq, BF16), jnp.asarray(sk, BF16)


def _fox_attn_body(q_ref, k_ref, vt_ref, cc_ref, sq_ref, sk_ref, y_ref,
                   kf_scr, qc_scr, st_scr, m_scr, ot_scr, yt_scr):
    qi = pl.program_id(1)
    seq = k_ref.shape[0]
    q0 = pl.multiple_of(qi * TQ, TQ)
    lane = lax.broadcasted_iota(jnp.int32, (TK, LANES), 1)

    @pl.when(qi == 0)
    def _():
        def chunk(r, carry):
            r0 = pl.multiple_of(r * TK, TK)
            ka = jnp.dot(_split_cum(cc_ref[pl.ds(r0, TK), :]), sk_ref[...],
                         preferred_element_type=F32).astype(BF16)
            for pair in range(N_HEADS // 2):
                cols = slice(pair * LANES, (pair + 1) * LANES)
                kp = k_ref[pl.ds(r0, TK), cols]
                kap = ka[:, cols]
                for hh in range(2):
                    own = (lane >= hh * HEAD_DIM) & (lane < (hh + 1) * HEAD_DIM)
                    own_b = (lane >= hh * BIAS_SLOT) & (lane < (hh + 1) * BIAS_SLOT)
                    rows = slice(hh * TK, (hh + 1) * TK)
                    kf_scr[pair, r, rows, :LANES] = jnp.where(own, kp, jnp.zeros_like(kp))
                    kf_scr[pair, r, rows, LANES:] = jnp.where(own_b, kap, jnp.zeros_like(kap))
            return carry
        lax.fori_loop(0, seq // TK, chunk, 0)

    qa = jnp.dot(_split_cum(cc_ref[pl.ds(q0, TQ), :]), sq_ref[...],
                 preferred_element_type=F32).astype(BF16)
    for pair in range(N_HEADS // 2):
        cols = slice(pair * LANES, (pair + 1) * LANES)
        qc_scr[pair, :, :LANES] = q_ref[:, cols]
        qc_scr[pair, :, LANES:] = qa[:, cols]
    m_scr[...] = jnp.full(m_scr.shape, NEG_BIG, F32)
    ot_scr[...] = jnp.zeros(ot_scr.shape, F32)

    k_idx = lax.broadcasted_iota(jnp.int32, (TK, TQ), 0)
    q_idx = lax.broadcasted_iota(jnp.int32, (TK, TQ), 1)

    n_pairs = N_HEADS // 2

    def scores(j, pair):
        st_scr[pair] = lax.dot_general(kf_scr[pair, j], qc_scr[pair], (((1,), (1,)), ((), ())),
                                       preferred_element_type=F32)

    def head(h, k0, on_diagonal):
        st = st_scr[h // 2, (h % 2) * TK:(h % 2 + 1) * TK, :]
        if on_diagonal:
            st = jnp.where(q_idx >= k_idx, st, NEG_BIG)
        m_old = m_scr[h:h + 1, :]
        m_new = jnp.maximum(m_old, jnp.max(st, axis=0, keepdims=True))
        alpha = jnp.exp(m_old - m_new)
        p = jnp.exp(st - m_new).astype(BF16)
        m_scr[h:h + 1, :] = m_new
        rows = slice(h * VT_ROWS, (h + 1) * VT_ROWS)
        ot_scr[rows, :] = alpha * ot_scr[rows, :] + jnp.dot(
            vt_ref[rows, pl.ds(k0, TK)], p, preferred_element_type=F32)

    def block(j, on_diagonal):
        k0 = pl.multiple_of(j * TK, TK)
        for pair in range(n_pairs):
            ahead = pair + SCORES_AHEAD
            if ahead < n_pairs:
                scores(j, ahead)
            elif not on_diagonal:
                scores(j + 1, ahead - n_pairs)
            head(2 * pair, k0, on_diagonal)
            head(2 * pair + 1, k0, on_diagonal)

    for pair in range(SCORES_AHEAD):
        scores(0, pair)

    def off_diag(j, carry):
        block(j, False)
        return carry
    lax.fori_loop(0, qi, off_diag, 0)
    block(qi, True)

    for h in range(N_HEADS):
        o = ot_scr[h * VT_ROWS:h * VT_ROWS + HEAD_DIM, :]
        l = ot_scr[h * VT_ROWS + HEAD_DIM:h * VT_ROWS + HEAD_DIM + 1, :]
        yt_scr[h * HEAD_DIM:(h + 1) * HEAD_DIM, :] = o / l
    y_ref[...] = yt_scr[...].T.astype(BF16)


def _mix_ffn2_body(x1_ref, y_ref, gm_ref, wgate_ref, cw_ref, woa_ref, woc_ref, wout_ref,
                   g3_ref, wg_ref, wu_ref, wd_ref, gf_ref,
                   o_ref, h_scr, act_scr, tail_scr):
    x1 = x1_ref[...]
    h_scr[...] = _rms(x1, gm_ref[...]).astype(BF16)

    def gate(lo, hi):
        return jnp.dot(h_scr[...], wgate_ref[:, lo:hi], preferred_element_type=F32)

    c_b = gate(0, CONV_W)
    u = gate(CONV_W, 2 * CONV_W) * gate(2 * CONV_W, 3 * CONV_W)

    @pl.when(pl.program_id(1) == 0)
    def _():
        tail_scr[...] = jnp.zeros_like(tail_scr)

    row = lax.broadcasted_iota(jnp.int32, u.shape, 0)
    prev1 = tail_scr[1:2, :]
    prev2 = tail_scr[0:1, :]
    u1 = jnp.where(row == 0, prev1, pltpu.roll(u, 1, axis=0))
    u2 = jnp.where(row == 0, prev2, jnp.where(row == 1, prev1, pltpu.roll(u, 2, axis=0)))
    tail_scr[...] = u[TM_C - 2:TM_C, :]
    conv = cw_ref[0:1, :] * u2 + cw_ref[1:2, :] * u1 + cw_ref[2:3, :] * u
    y_conv = jnp.dot((c_b * conv).astype(BF16), woc_ref[...], preferred_element_type=F32)
    y_attn = jnp.dot(y_ref[...], woa_ref[...], preferred_element_type=F32)

    o0 = 3 * CONV_W
    merged = (jax.nn.sigmoid(gate(o0, o0 + D_MODEL)) * y_attn
              + jax.nn.sigmoid(gate(o0 + D_MODEL, o0 + 2 * D_MODEL)) * y_conv)
    x2 = x1 + jnp.dot(merged.astype(BF16), wout_ref[...], preferred_element_type=F32)

    h_scr[...] = _rms(x2, g3_ref[...]).astype(BF16)
    x3 = x2 + FFN_RES * _swiglu(h_scr, act_scr, wg_ref, wu_ref, wd_ref)
    o_ref[...] = _rms(x3, gf_ref[...])


def _resident(shape):
    return pl.BlockSpec(shape, lambda b, i: (0,) * len(shape), pipeline_mode=pl.Buffered(1))


def kernel(x, ffn1_norm, ffn1_gate, ffn1_up, ffn1_down, mix_norm, w_in, b_forget, conv_w,
           w_o_attn, w_o_conv, w_out, ffn2_norm, ffn2_gate, ffn2_up, ffn2_down, final_norm):
    bsz, seq, d = x.shape
    assert d == D_MODEL and seq % TM_A == 0 and seq % TM_C == 0 and seq % TQ == 0
    row = lambda v: v.reshape(1, -1).astype(F32)
    n_qkvf = 3 * ATTN_W + N_HEADS
    w_a = jnp.pad(w_in[:, :n_qkvf], ((0, 0), (0, QKVF_COLS - n_qkvf))).astype(BF16)
    w_gate = w_in[:, n_qkvf:].astype(BF16)
    b_f = jnp.pad(row(b_forget), ((0, 0), (0, LANES - N_HEADS)))
    sel_q, sel_k = _bias_selectors()
    params = pltpu.CompilerParams(dimension_semantics=("arbitrary", "arbitrary"),
                                  vmem_limit_bytes=VMEM_LIMIT)

    tile_a = lambda w: pl.BlockSpec((None, TM_A, w), lambda b, i: (b, i, 0))
    x1, qk, vt, cum = pl.pallas_call(
        _ffn1_qkv_body,
        grid=(bsz, seq // TM_A),
        in_specs=[tile_a(D_MODEL), _resident((1, D_MODEL)),
                  _resident((D_MODEL, D_FF)), _resident((D_MODEL, D_FF)),
                  _resident((D_FF, D_MODEL)), _resident((1, D_MODEL)),
                  _resident((D_MODEL, QKVF_COLS)), _resident((1, LANES))],
        out_specs=[tile_a(D_MODEL), tile_a(2 * ATTN_W),
                   pl.BlockSpec((None, N_HEADS * VT_ROWS, TM_A), lambda b, i: (b, 0, i)),
                   tile_a(LANES)],
        out_shape=[jax.ShapeDtypeStruct((bsz, seq, D_MODEL), F32),
                   jax.ShapeDtypeStruct((bsz, seq, 2 * ATTN_W), BF16),
                   jax.ShapeDtypeStruct((bsz, N_HEADS * VT_ROWS, seq), BF16),
                   jax.ShapeDtypeStruct((bsz, seq, LANES), F32)],
        scratch_shapes=[pltpu.VMEM((TM_A, D_MODEL), BF16), pltpu.VMEM((TM_A, D_FF), BF16),
                        pltpu.VMEM((1, LANES), F32)],
        compiler_params=params,
        name="ffn1_qkv",
    )(x, row(ffn1_norm), ffn1_gate.astype(BF16), ffn1_up.astype(BF16),
      ffn1_down.astype(BF16), row(mix_norm), w_a, b_f)

    y = pl.pallas_call(
        _fox_attn_body,
        grid=(bsz, seq // TQ),
        in_specs=[pl.BlockSpec((None, TQ, ATTN_W), lambda b, i: (b, i, 0)),
                  pl.BlockSpec((None, seq, ATTN_W), lambda b, i: (b, 0, 1)),
                  pl.BlockSpec((None, N_HEADS * VT_ROWS, seq), lambda b, i: (b, 0, 0)),
                  pl.BlockSpec((None, seq, LANES), lambda b, i: (b, 0, 0)),
                  _resident((N_SPLIT * LANES, ATTN_W)), _resident((N_SPLIT * LANES, ATTN_W))],
        out_specs=pl.BlockSpec((None, TQ, ATTN_W), lambda b, i: (b, i, 0)),
        out_shape=jax.ShapeDtypeStruct((bsz, seq, ATTN_W), BF16),
        scratch_shapes=[pltpu.VMEM((N_HEADS // 2, seq // TK, 2 * TK, 2 * LANES), BF16),
                        pltpu.VMEM((N_HEADS // 2, TQ, 2 * LANES), BF16),
                        pltpu.VMEM((N_HEADS // 2, 2 * TK, TQ), F32),
                        pltpu.VMEM((N_HEADS, TQ), F32),
                        pltpu.VMEM((N_HEADS * VT_ROWS, TQ), F32),
                        pltpu.VMEM((ATTN_W, TQ), F32)],
        compiler_params=params,
        name="fox_attn",
    )(qk, qk, vt, cum, sel_q, sel_k)

    tile_c = lambda w: pl.BlockSpec((None, TM_C, w), lambda b, i: (b, i, 0))
    return pl.pallas_call(
        _mix_ffn2_body,
        grid=(bsz, seq // TM_C),
        in_specs=[tile_c(D_MODEL), tile_c(ATTN_W), _resident((1, D_MODEL)),
                  _resident((D_MODEL, GATE_COLS)), _resident((3, CONV_W)),
                  _resident((ATTN_W, D_MODEL)), _resident((CONV_W, D_MODEL)),
                  _resident((D_MODEL, D_MODEL)), _resident((1, D_MODEL)),
                  _resident((D_MODEL, D_FF)), _resident((D_MODEL, D_FF)),
                  _resident((D_FF, D_MODEL)), _resident((1, D_MODEL))],
        out_specs=tile_c(D_MODEL),
        out_shape=jax.ShapeDtypeStruct((bsz, seq, D_MODEL), F32),
        scratch_shapes=[pltpu.VMEM((TM_C, D_MODEL), BF16), pltpu.VMEM((TM_C, D_FF), BF16),
                        pltpu.VMEM((2, CONV_W), F32)],
        compiler_params=params,
        name="mix_ffn2",
    )(x1, y, row(mix_norm), w_gate, conv_w.astype(F32), w_o_attn.astype(BF16),
      w_o_conv.astype(BF16), w_out.astype(BF16), row(ffn2_norm), ffn2_gate.astype(BF16),
      ffn2_up.astype(BF16), ffn2_down.astype(BF16), row(final_norm))
```

```python
import math

import numpy as np
import jax
import jax.numpy as jnp
from jax import lax
from jax.experimental import pallas as pl
from jax.experimental.pallas import tpu as pltpu

F32 = jnp.float32
BF16 = jnp.bfloat16

D_MODEL = 1024
N_HEADS = 8
HEAD_DIM = 64
ATTN_W = N_HEADS * HEAD_DIM
CONV_W = D_MODEL // 2
D_FF = 2816
RMS_EPS = 1e-6
FFN_RES = 0.5
LANES = 128
QKVF_COLS = 3 * ATTN_W + LANES
GATE_COLS = 3 * CONV_W + 2 * D_MODEL
FF_CHUNK = 256
TM_A = 512
TM_C = 512
TQ = 256
TK = TQ
SCORES_AHEAD = 3
LOG2E = 1.4426950408889634
NEG_BIG = -1e30
VT_ROWS = HEAD_DIM + 16
N_SPLIT = 3
BIAS_SLOT = 8
ONES_LANE = N_HEADS
VMEM_LIMIT = 56 * 1024 * 1024


def _rms(x, g):
    inv = lax.rsqrt(jnp.mean(x * x, axis=-1, keepdims=True) + RMS_EPS)
    return (x * inv) * g


def _swiglu(h_scr, act_scr, wg_ref, wu_ref, wd_ref):
    for c in range(D_FF // FF_CHUNK):
        sl = slice(c * FF_CHUNK, (c + 1) * FF_CHUNK)
        g = jnp.dot(h_scr[...], wg_ref[:, sl], preferred_element_type=F32)
        u = jnp.dot(h_scr[...], wu_ref[:, sl], preferred_element_type=F32)
        act_scr[:, sl] = (g * jax.nn.sigmoid(g) * u).astype(BF16)
    return jnp.dot(act_scr[...], wd_ref[...], preferred_element_type=F32)


def _cumsum_rows(x):
    n = x.shape[0]
    row = lax.broadcasted_iota(jnp.int32, x.shape, 0)
    d = 1
    while d < n:
        x = x + jnp.where(row >= d, pltpu.roll(x, d, axis=0), 0.0)
        d *= 2
    return x


def _ffn1_qkv_body(x_ref, g1_ref, wg_ref, wu_ref, wd_ref, g2_ref, wa_ref, bf_ref,
                   x1_ref, qk_ref, vt_ref, cc_ref,
                   h_scr, act_scr, carry_scr):
    x = x_ref[...]
    h_scr[...] = _rms(x, g1_ref[...]).astype(BF16)
    x1 = x + FFN_RES * _swiglu(h_scr, act_scr, wg_ref, wu_ref, wd_ref)
    x1_ref[...] = x1

    h2 = _rms(x1, g2_ref[...]).astype(BF16)
    pr = jnp.dot(h2, wa_ref[...], preferred_element_type=F32)
    qk_ref[:, :ATTN_W] = (pr[:, :ATTN_W] * (LOG2E / math.sqrt(HEAD_DIM))).astype(BF16)
    qk_ref[:, ATTN_W:] = pr[:, ATTN_W:2 * ATTN_W].astype(BF16)
    v_t = pr[:, 2 * ATTN_W:3 * ATTN_W].T.astype(BF16)
    for h in range(N_HEADS):
        vt_ref[h * VT_ROWS:h * VT_ROWS + HEAD_DIM, :] = v_t[h * HEAD_DIM:(h + 1) * HEAD_DIM, :]
        vt_ref[h * VT_ROWS + HEAD_DIM:(h + 1) * VT_ROWS, :] = jnp.ones(
            (VT_ROWS - HEAD_DIM, TM_A), BF16)

    @pl.when(pl.program_id(1) == 0)
    def _():
        carry_scr[...] = jnp.zeros_like(carry_scr)

    z = pr[:, 3 * ATTN_W:] + bf_ref[...]
    log_f = jnp.minimum(z, 0.0) - jnp.log1p(jnp.exp(-jnp.abs(z)))
    cum = _cumsum_rows(log_f) + carry_scr[...]
    carry_scr[...] = cum[TM_A - 1:TM_A, :]
    cc_ref[...] = cum * LOG2E


def _split_cum(c):
    pieces = []
    r = c
    for _ in range(N_SPLIT):
        p = r.astype(BF16)
        pieces.append(p)
        r = r - p.astype(F32)
    lane = lax.broadcasted_iota(jnp.int32, c.shape, 1)
    pieces[0] = jnp.where(lane == ONES_LANE, jnp.ones_like(pieces[0]), pieces[0])
    return jnp.concatenate(pieces, axis=1)


def _bias_selectors():
    sq = np.zeros((N_SPLIT * LANES, ATTN_W), np.float32)
    sk = np.zeros((N_SPLIT * LANES, ATTN_W), np.float32)
    for h in range(N_HEADS):
        base = (h // 2) * LANES + (h % 2) * BIAS_SLOT
        for s in range(N_SPLIT):
            sq[s * LANES + h, base + s] = 1.0
            sq[ONES_LANE, base + N_SPLIT + s] = 1.0
            sk[ONES_LANE, base + s] = 1.0
            sk[s * LANES + h, base + N_SPLIT + s] = -1.0
    return jnp.asarray(sq, BF16), jnp.asarray(sk, BF16)


def _fox_attn_body(q_ref, k_ref, vt_ref, cc_ref, sq_ref, sk_ref, y_ref,
                   kf_scr, qc_scr, st_scr, m_scr, ot_scr, yt_scr):
    qi = pl.program_id(1)
    seq = k_ref.shape[0]
    q0 = pl.multiple_of(qi * TQ, TQ)
    lane = lax.broadcasted_iota(jnp.int32, (TK, LANES), 1)

    @pl.when(qi == 0)
    def _():
        for r in range(seq // TK):
            r0 = r * TK
            ka = jnp.dot(_split_cum(cc_ref[pl.ds(r0, TK), :]), sk_ref[...],
                         preferred_element_type=F32).astype(BF16)
            for pair in range(N_HEADS // 2):
                cols = slice(pair * LANES, (pair + 1) * LANES)
                kp = k_ref[pl.ds(r0, TK), cols]
                kap = ka[:, cols]
                for hh in range(2):
                    own = (lane >= hh * HEAD_DIM) & (lane < (hh + 1) * HEAD_DIM)
                    own_b = (lane >= hh * BIAS_SLOT) & (lane < (hh + 1) * BIAS_SLOT)
                    rows = slice(hh * TK, (hh + 1) * TK)
                    kf_scr[pair, r, rows, :LANES] = jnp.where(own, kp, jnp.zeros_like(kp))
                    kf_scr[pair, r, rows, LANES:] = jnp.where(own_b, kap, jnp.zeros_like(kap))

    qa = jnp.dot(_split_cum(cc_ref[pl.ds(q0, TQ), :]), sq_ref[...],
                 preferred_element_type=F32).astype(BF16)
    for pair in range(N_HEADS // 2):
        cols = slice(pair * LANES, (pair + 1) * LANES)
        qc_scr[pair, :, :LANES] = q_ref[:, cols]
        qc_scr[pair, :, LANES:] = qa[:, cols]
    m_scr[...] = jnp.full(m_scr.shape, NEG_BIG, F32)
    ot_scr[...] = jnp.zeros(ot_scr.shape, F32)

    k_idx = lax.broadcasted_iota(jnp.int32, (TK, TQ), 0)
    q_idx = lax.broadcasted_iota(jnp.int32, (TK, TQ), 1)

    n_pairs = N_HEADS // 2

    def scores(j, pair):
        st_scr[pair] = lax.dot_general(kf_scr[pair, j], qc_scr[pair], (((1,), (1,)), ((), ())),
                                       preferred_element_type=F32)

    def head(h, k0, on_diagonal):
        st = st_scr[h // 2, (h % 2) * TK:(h % 2 + 1) * TK, :]
        if on_diagonal:
            st = jnp.where(q_idx >= k_idx, st, NEG_BIG)
        m_old = m_scr[h:h + 1, :]
        m_new = jnp.maximum(m_old, jnp.max(st, axis=0, keepdims=True))
        alpha = jnp.exp2(m_old - m_new)
        p = jnp.exp2(st - m_new).astype(BF16)
        m_scr[h:h + 1, :] = m_new
        rows = slice(h * VT_ROWS, (h + 1) * VT_ROWS)
        ot_scr[rows, :] = alpha * ot_scr[rows, :] + jnp.dot(
            vt_ref[rows, pl.ds(k0, TK)], p, preferred_element_type=F32)

    def block(j, on_diagonal):
        k0 = pl.multiple_of(j * TK, TK)
        for pair in range(n_pairs):
            ahead = pair + SCORES_AHEAD
            if ahead < n_pairs:
                scores(j, ahead)
            elif not on_diagonal:
                scores(j + 1, ahead - n_pairs)
            head(2 * pair, k0, on_diagonal)
            head(2 * pair + 1, k0, on_diagonal)

    for pair in range(SCORES_AHEAD):
        scores(0, pair)

    def off_diag(j, carry):
        block(j, False)
        return carry
    lax.fori_loop(0, qi, off_diag, 0)
    block(qi, True)

    for h in range(N_HEADS):
        o = ot_scr[h * VT_ROWS:h * VT_ROWS + HEAD_DIM, :]
        l = ot_scr[h * VT_ROWS + HEAD_DIM:h * VT_ROWS + HEAD_DIM + 1, :]
        yt_scr[h * HEAD_DIM:(h + 1) * HEAD_DIM, :] = o / l
    y_ref[...] = yt_scr[...].T.astype(BF16)


def _mix_ffn2_body(x1_ref, y_ref, gm_ref, wgate_ref, cw_ref, woa_ref, woc_ref, wout_ref,
                   g3_ref, wg_ref, wu_ref, wd_ref, gf_ref,
                   o_ref, h_scr, act_scr, tail_scr):
    x1 = x1_ref[...]
    h_scr[...] = _rms(x1, gm_ref[...]).astype(BF16)

    def gate(lo, hi):
        return jnp.dot(h_scr[...], wgate_ref[:, lo:hi], preferred_element_type=F32)

    c_b = gate(0, CONV_W)
    u = gate(CONV_W, 2 * CONV_W) * gate(2 * CONV_W, 3 * CONV_W)

    @pl.when(pl.program_id(1) == 0)
    def _():
        tail_scr[...] = jnp.zeros_like(tail_scr)

    row = lax.broadcasted_iota(jnp.int32, u.shape, 0)
    prev1 = tail_scr[1:2, :]
    prev2 = tail_scr[0:1, :]
    u1 = jnp.where(row == 0, prev1, pltpu.roll(u, 1, axis=0))
    u2 = jnp.where(row == 0, prev2, jnp.where(row == 1, prev1, pltpu.roll(u, 2, axis=0)))
    tail_scr[...] = u[TM_C - 2:TM_C, :]
    conv = cw_ref[0:1, :] * u2 + cw_ref[1:2, :] * u1 + cw_ref[2:3, :] * u
    y_conv = jnp.dot((c_b * conv).astype(BF16), woc_ref[...], preferred_element_type=F32)
    y_attn = jnp.dot(y_ref[...], woa_ref[...], preferred_element_type=F32)

    o0 = 3 * CONV_W
    merged = (jax.nn.sigmoid(gate(o0, o0 + D_MODEL)) * y_attn
              + jax.nn.sigmoid(gate(o0 + D_MODEL, o0 + 2 * D_MODEL)) * y_conv)
    x2 = x1 + jnp.dot(merged.astype(BF16), wout_ref[...], preferred_element_type=F32)

    h_scr[...] = _rms(x2, g3_ref[...]).astype(BF16)
    x3 = x2 + FFN_RES * _swiglu(h_scr, act_scr, wg_ref, wu_ref, wd_ref)
    o_ref[...] = _rms(x3, gf_ref[...])


def _resident(shape):
    return pl.BlockSpec(shape, lambda b, i: (0,) * len(shape), pipeline_mode=pl.Buffered(1))


def kernel(x, ffn1_norm, ffn1_gate, ffn1_up, ffn1_down, mix_norm, w_in, b_forget, conv_w,
           w_o_attn, w_o_conv, w_out, ffn2_norm, ffn2_gate, ffn2_up, ffn2_down, final_norm):
    bsz, seq, d = x.shape
    assert d == D_MODEL and seq % TM_A == 0 and seq % TM_C == 0 and seq % TQ == 0
    row = lambda v: v.reshape(1, -1).astype(F32)
    n_qkvf = 3 * ATTN_W + N_HEADS
    w_a = jnp.pad(w_in[:, :n_qkvf], ((0, 0), (0, QKVF_COLS - n_qkvf))).astype(BF16)
    w_gate = w_in[:, n_qkvf:].astype(BF16)
    b_f = jnp.pad(row(b_forget), ((0, 0), (0, LANES - N_HEADS)))
    sel_q, sel_k = _bias_selectors()
    params = pltpu.CompilerParams(dimension_semantics=("arbitrary", "arbitrary"),
                                  vmem_limit_bytes=VMEM_LIMIT)

    tile_a = lambda w: pl.BlockSpec((None, TM_A, w), lambda b, i: (b, i, 0))
    x1, qk, vt, cum = pl.pallas_call(
        _ffn1_qkv_body,
        grid=(bsz, seq // TM_A),
        in_specs=[tile_a(D_MODEL), _resident((1, D_MODEL)),
                  _resident((D_MODEL, D_FF)), _resident((D_MODEL, D_FF)),
                  _resident((D_FF, D_MODEL)), _resident((1, D_MODEL)),
                  _resident((D_MODEL, QKVF_COLS)), _resident((1, LANES))],
        out_specs=[tile_a(D_MODEL), tile_a(2 * ATTN_W),
                   pl.BlockSpec((None, N_HEADS * VT_ROWS, TM_A), lambda b, i: (b, 0, i)),
                   tile_a(LANES)],
        out_shape=[jax.ShapeDtypeStruct((bsz, seq, D_MODEL), F32),
                   jax.ShapeDtypeStruct((bsz, seq, 2 * ATTN_W), BF16),
                   jax.ShapeDtypeStruct((bsz, N_HEADS * VT_ROWS, seq), BF16),
                   jax.ShapeDtypeStruct((bsz, seq, LANES), F32)],
        scratch_shapes=[pltpu.VMEM((TM_A, D_MODEL), BF16), pltpu.VMEM((TM_A, D_FF), BF16),
                        pltpu.VMEM((1, LANES), F32)],
        compiler_params=params,
        name="ffn1_qkv",
    )(x, row(ffn1_norm), ffn1_gate.astype(BF16), ffn1_up.astype(BF16),
      ffn1_down.astype(BF16), row(mix_norm), w_a, b_f)

    y = pl.pallas_call(
        _fox_attn_body,
        grid=(bsz, seq // TQ),
        in_specs=[pl.BlockSpec((None, TQ, ATTN_W), lambda b, i: (b, i, 0)),
                  pl.BlockSpec((None, seq, ATTN_W), lambda b, i: (b, 0, 1)),
                  pl.BlockSpec((None, N_HEADS * VT_ROWS, seq), lambda b, i: (b, 0, 0)),
                  pl.BlockSpec((None, seq, LANES), lambda b, i: (b, 0, 0)),
                  _resident((N_SPLIT * LANES, ATTN_W)), _resident((N_SPLIT * LANES, ATTN_W))],
        out_specs=pl.BlockSpec((None, TQ, ATTN_W), lambda b, i: (b, i, 0)),
        out_shape=jax.ShapeDtypeStruct((bsz, seq, ATTN_W), BF16),
        scratch_shapes=[pltpu.VMEM((N_HEADS // 2, seq // TK, 2 * TK, 2 * LANES), BF16),
                        pltpu.VMEM((N_HEADS // 2, TQ, 2 * LANES), BF16),
                        pltpu.VMEM((N_HEADS // 2, 2 * TK, TQ), F32),
                        pltpu.VMEM((N_HEADS, TQ), F32),
                        pltpu.VMEM((N_HEADS * VT_ROWS, TQ), F32),
                        pltpu.VMEM((ATTN_W, TQ), F32)],
        compiler_params=params,
        name="fox_attn",
    )(qk, qk, vt, cum, sel_q, sel_k)

    tile_c = lambda w: pl.BlockSpec((None, TM_C, w), lambda b, i: (b, i, 0))
    return pl.pallas_call(
        _mix_ffn2_body,
        grid=(bsz, seq // TM_C),
        in_specs=[tile_c(D_MODEL), tile_c(ATTN_W), _resident((1, D_MODEL)),
                  _resident((D_MODEL, GATE_COLS)), _resident((3, CONV_W)),
                  _resident((ATTN_W, D_MODEL)), _resident((CONV_W, D_MODEL)),
                  _resident((D_MODEL, D_MODEL)), _resident((1, D_MODEL)),
                  _resident((D_MODEL, D_FF)), _resident((D_MODEL, D_FF)),
                  _resident((D_FF, D_MODEL)), _resident((1, D_MODEL))],
        out_specs=tile_c(D_MODEL),
        out_shape=jax.ShapeDtypeStruct((bsz, seq, D_MODEL), F32),
        scratch_shapes=[pltpu.VMEM((TM_C, D_MODEL), BF16), pltpu.VMEM((TM_C, D_FF), BF16),
                        pltpu.VMEM((2, CONV_W), F32)],
        compiler_params=params,
        name="mix_ffn2",
    )(x1, y, row(mix_norm), w_gate, conv_w.astype(F32), w_o_attn.astype(BF16),
      w_o_conv.astype(BF16), w_out.astype(BF16), row(ffn2_norm), ffn2_gate.astype(BF16),
      ffn2_up.astype(BF16), ffn2_down.astype(BF16), row(final_norm))
```

```python
import math

import numpy as np
import jax
import jax.numpy as jnp
from jax import lax
from jax.experimental import pallas as pl
from jax.experimental.pallas import tpu as pltpu

F32 = jnp.float32
BF16 = jnp.bfloat16

D_MODEL = 1024
N_HEADS = 8
HEAD_DIM = 64
ATTN_W = N_HEADS * HEAD_DIM
CONV_W = D_MODEL // 2
D_FF = 2816
RMS_EPS = 1e-6
FFN_RES = 0.5
LANES = 128
QKVF_COLS = 3 * ATTN_W + LANES
GATE_COLS = 3 * CONV_W + 2 * D_MODEL
FF_CHUNK = 256
TM_A = 512
TM_C = 512
SUB_TILES = 2
TQ = 256
TK = TQ
SCORES_AHEAD = 3
LOG2E = 1.4426950408889634
NEG_BIG = -1e30
VT_ROWS = HEAD_DIM + 16
N_SPLIT = 3
BIAS_SLOT = 8
ONES_LANE = N_HEADS
VMEM_LIMIT = 56 * 1024 * 1024


def _rms(x, g):
    inv = lax.rsqrt(jnp.mean(x * x, axis=-1, keepdims=True) + RMS_EPS)
    return (x * inv) * g


def _sub_rows(tm):
    th = tm // SUB_TILES
    return [slice(i * th, (i + 1) * th) for i in range(SUB_TILES)]


def _swiglu_act(h_scr, act_scr, wg_ref, wu_ref, subs, fill_h=None):
    for c in range(D_FF // FF_CHUNK):
        sl = slice(c * FF_CHUNK, (c + 1) * FF_CHUNK)
        for rows in subs:
            if c == 0 and fill_h is not None:
                fill_h(rows)
            g = jnp.dot(h_scr[rows, :], wg_ref[:, sl], preferred_element_type=F32)
            u = jnp.dot(h_scr[rows, :], wu_ref[:, sl], preferred_element_type=F32)
            act_scr[rows, sl] = (g * jax.nn.sigmoid(g) * u).astype(BF16)


def _cumsum_rows(x):
    n = x.shape[0]
    row = lax.broadcasted_iota(jnp.int32, x.shape, 0)
    d = 1
    while d < n:
        x = x + jnp.where(row >= d, pltpu.roll(x, d, axis=0), 0.0)
        d *= 2
    return x


def _ffn1_qkv_body(x_ref, g1_ref, wg_ref, wu_ref, wd_ref, g2_ref, wa_ref, bf_ref,
                   x1_ref, qk_ref, vt_ref, cc_ref,
                   h_scr, act_scr, carry_scr):
    subs = _sub_rows(TM_A)
    def fill_h(rows):
        h_scr[rows, :] = _rms(x_ref[rows, :], g1_ref[...]).astype(BF16)
    _swiglu_act(h_scr, act_scr, wg_ref, wu_ref, subs, fill_h)
    for rows in subs:
        x1 = x_ref[rows, :] + FFN_RES * jnp.dot(act_scr[rows, :], wd_ref[...],
                                                preferred_element_type=F32)
        x1_ref[rows, :] = x1
        h_scr[rows, :] = _rms(x1, g2_ref[...]).astype(BF16)

    @pl.when(pl.program_id(1) == 0)
    def _():
        carry_scr[...] = jnp.zeros_like(carry_scr)

    carry = carry_scr[...]
    for rows in subs:
        pr = jnp.dot(h_scr[rows, :], wa_ref[...], preferred_element_type=F32)
        qk_ref[rows, :ATTN_W] = (pr[:, :ATTN_W] * (LOG2E / math.sqrt(HEAD_DIM))).astype(BF16)
        qk_ref[rows, ATTN_W:] = pr[:, ATTN_W:2 * ATTN_W].astype(BF16)
        v_t = pr[:, 2 * ATTN_W:3 * ATTN_W].T.astype(BF16)
        for h in range(N_HEADS):
            vt_ref[h * VT_ROWS:h * VT_ROWS + HEAD_DIM, rows] = v_t[h * HEAD_DIM:(h + 1) * HEAD_DIM, :]
            vt_ref[h * VT_ROWS + HEAD_DIM:(h + 1) * VT_ROWS, rows] = jnp.ones(
                (VT_ROWS - HEAD_DIM, v_t.shape[1]), BF16)
        z = pr[:, 3 * ATTN_W:] + bf_ref[...]
        log_f = jnp.minimum(z, 0.0) - jnp.log1p(jnp.exp(-jnp.abs(z)))
        cum = _cumsum_rows(log_f) + carry
        carry = cum[cum.shape[0] - 1:, :]
        cc_ref[rows, :] = cum * LOG2E
    carry_scr[...] = carry


def _split_cum(c):
    pieces = []
    r = c
    for _ in range(N_SPLIT):
        p = r.astype(BF16)
        pieces.append(p)
        r = r - p.astype(F32)
    lane = lax.broadcasted_iota(jnp.int32, c.shape, 1)
    pieces[0] = jnp.where(lane == ONES_LANE, jnp.ones_like(pieces[0]), pieces[0])
    return jnp.concatenate(pieces, axis=1)


def _bias_selectors():
    sq = np.zeros((N_SPLIT * LANES, ATTN_W), np.float32)
    sk = np.zeros((N_SPLIT * LANES, ATTN_W), np.float32)
    for h in range(N_HEADS):
        base = (h // 2) * LANES + (h % 2) * BIAS_SLOT
        for s in range(N_SPLIT):
            sq[s * LANES + h, base + s] = 1.0
            sq[ONES_LANE, base + N_SPLIT + s] = 1.0
            sk[ONES_LANE, base + s] = 1.0
            sk[s * LANES + h, base + N_SPLIT + s] = -1.0
    return jnp.asarray(sq, BF16), jnp.asarray(sk, BF16)


def _fox_attn_body(q_ref, k_ref, vt_ref, cc_ref, sq_ref, sk_ref, y_ref,
                   kf_scr, qc_scr, st_scr, m_scr, ot_scr, yt_scr):
    qi = pl.program_id(1)
    seq = k_ref.shape[0]
    q0 = pl.multiple_of(qi * TQ, TQ)
    lane = lax.broadcasted_iota(jnp.int32, (TK, LANES), 1)

    @pl.when(qi == 0)
    def _():
        for r in range(seq // TK):
            r0 = r * TK
            ka = jnp.dot(_split_cum(cc_ref[pl.ds(r0, TK), :]), sk_ref[...],
                         preferred_element_type=F32).astype(BF16)
            for pair in range(N_HEADS // 2):
                cols = slice(pair * LANES, (pair + 1) * LANES)
                kp = k_ref[pl.ds(r0, TK), cols]
                kap = ka[:, cols]
                for hh in range(2):
                    own = (lane >= hh * HEAD_DIM) & (lane < (hh + 1) * HEAD_DIM)
                    own_b = (lane >= hh * BIAS_SLOT) & (lane < (hh + 1) * BIAS_SLOT)
                    rows = slice(hh * TK, (hh + 1) * TK)
                    kf_scr[pair, r, rows, :LANES] = jnp.where(own, kp, jnp.zeros_like(kp))
                    kf_scr[pair, r, rows, LANES:] = jnp.where(own_b, kap, jnp.zeros_like(kap))

    qa = jnp.dot(_split_cum(cc_ref[pl.ds(q0, TQ), :]), sq_ref[...],
                 preferred_element_type=F32).astype(BF16)
    for pair in range(N_HEADS // 2):
        cols = slice(pair * LANES, (pair + 1) * LANES)
        qc_scr[pair, :, :LANES] = q_ref[:, cols]
        qc_scr[pair, :, LANES:] = qa[:, cols]
    m_scr[...] = jnp.full(m_scr.shape, NEG_BIG, F32)
    ot_scr[...] = jnp.zeros(ot_scr.shape, F32)

    k_idx = lax.broadcasted_iota(jnp.int32, (TK, TQ), 0)
    q_idx = lax.broadcasted_iota(jnp.int32, (TK, TQ), 1)

    n_pairs = N_HEADS // 2

    def scores(j, pair):
        st_scr[pair] = lax.dot_general(kf_scr[pair, j], qc_scr[pair], (((1,), (1,)), ((), ())),
                                       preferred_element_type=F32)

    def head(h, k0, on_diagonal):
        st = st_scr[h // 2, (h % 2) * TK:(h % 2 + 1) * TK, :]
        if on_diagonal:
            st = jnp.where(q_idx >= k_idx, st, NEG_BIG)
        m_old = m_scr[h:h + 1, :]
        m_new = jnp.maximum(m_old, jnp.max(st, axis=0, keepdims=True))
        alpha = jnp.exp2(m_old - m_new)
        p = jnp.exp2(st - m_new).astype(BF16)
        m_scr[h:h + 1, :] = m_new
        rows = slice(h * VT_ROWS, (h + 1) * VT_ROWS)
        ot_scr[rows, :] = alpha * ot_scr[rows, :] + jnp.dot(
            vt_ref[rows, pl.ds(k0, TK)], p, preferred_element_type=F32)

    def block(j, on_diagonal):
        k0 = pl.multiple_of(j * TK, TK)
        for pair in range(n_pairs):
            ahead = pair + SCORES_AHEAD
            if ahead < n_pairs:
                scores(j, ahead)
            elif not on_diagonal:
                scores(j + 1, ahead - n_pairs)
            head(2 * pair, k0, on_diagonal)
            head(2 * pair + 1, k0, on_diagonal)

    for pair in range(SCORES_AHEAD):
        scores(0, pair)

    def off_diag(j, carry):
        block(j, False)
        return carry
    lax.fori_loop(0, qi, off_diag, 0)
    block(qi, True)

    for h in range(N_HEADS):
        o = ot_scr[h * VT_ROWS:h * VT_ROWS + HEAD_DIM, :]
        l = ot_scr[h * VT_ROWS + HEAD_DIM:h * VT_ROWS + HEAD_DIM + 1, :]
        yt_scr[h * HEAD_DIM:(h + 1) * HEAD_DIM, :] = o / l
    y_ref[...] = yt_scr[...].T.astype(BF16)


def _mix_ffn2_body(x1_ref, y_ref, gm_ref, wgate_ref, cw_ref, woa_ref, woc_ref, wout_ref,
                   g3_ref, wg_ref, wu_ref, wd_ref, gf_ref,
                   o_ref, h_scr, act_scr, tail_scr):
    subs = _sub_rows(TM_C)
    th = TM_C // SUB_TILES

    def gate(rows, lo, hi):
        return jnp.dot(h_scr[rows, :], wgate_ref[:, lo:hi], preferred_element_type=F32)

    @pl.when(pl.program_id(1) == 0)
    def _():
        tail_scr[...] = jnp.zeros_like(tail_scr)

    row = lax.broadcasted_iota(jnp.int32, (th, CONV_W), 0)
    tail = tail_scr[...]
    mixed = []
    for rows in subs:
        h_scr[rows, :] = _rms(x1_ref[rows, :], gm_ref[...]).astype(BF16)
        c_b = gate(rows, 0, CONV_W)
        u = gate(rows, CONV_W, 2 * CONV_W) * gate(rows, 2 * CONV_W, 3 * CONV_W)
        prev2, prev1 = tail[0:1, :], tail[1:2, :]
        u1 = jnp.where(row == 0, prev1, pltpu.roll(u, 1, axis=0))
        u2 = jnp.where(row == 0, prev2, jnp.where(row == 1, prev1, pltpu.roll(u, 2, axis=0)))
        tail = u[th - 2:th, :]
        conv = cw_ref[0:1, :] * u2 + cw_ref[1:2, :] * u1 + cw_ref[2:3, :] * u
        mixed.append((c_b * conv).astype(BF16))
    tail_scr[...] = tail

    o0 = 3 * CONV_W
    for rows, z in zip(subs, mixed):
        y_conv = jnp.dot(z, woc_ref[...], preferred_element_type=F32)
        y_attn = jnp.dot(y_ref[rows, :], woa_ref[...], preferred_element_type=F32)
        merged = (jax.nn.sigmoid(gate(rows, o0, o0 + D_MODEL)) * y_attn
                  + jax.nn.sigmoid(gate(rows, o0 + D_MODEL, o0 + 2 * D_MODEL)) * y_conv)
        act_scr[rows, :D_MODEL] = merged.astype(BF16)
    for rows in subs:
        x2 = x1_ref[rows, :] + jnp.dot(act_scr[rows, :D_MODEL], wout_ref[...],
                                       preferred_element_type=F32)
        o_ref[rows, :] = x2
        h_scr[rows, :] = _rms(x2, g3_ref[...]).astype(BF16)

    _swiglu_act(h_scr, act_scr, wg_ref, wu_ref, subs)
    for rows in subs:
        x3 = o_ref[rows, :] + FFN_RES * jnp.dot(act_scr[rows, :], wd_ref[...],
                                                preferred_element_type=F32)
        o_ref[rows, :] = _rms(x3, gf_ref[...])


def _resident(shape):
    return pl.BlockSpec(shape, lambda b, i: (0,) * len(shape), pipeline_mode=pl.Buffered(1))


def kernel(x, ffn1_norm, ffn1_gate, ffn1_up, ffn1_down, mix_norm, w_in, b_forget, conv_w,
           w_o_attn, w_o_conv, w_out, ffn2_norm, ffn2_gate, ffn2_up, ffn2_down, final_norm):
    bsz, seq, d = x.shape
    assert d == D_MODEL and seq % TM_A == 0 and seq % TM_C == 0 and seq % TQ == 0
    row = lambda v: v.reshape(1, -1).astype(F32)
    n_qkvf = 3 * ATTN_W + N_HEADS
    w_a = jnp.pad(w_in[:, :n_qkvf], ((0, 0), (0, QKVF_COLS - n_qkvf))).astype(BF16)
    w_gate = w_in[:, n_qkvf:].astype(BF16)
    b_f = jnp.pad(row(b_forget), ((0, 0), (0, LANES - N_HEADS)))
    sel_q, sel_k = _bias_selectors()
    params = pltpu.CompilerParams(dimension_semantics=("arbitrary", "arbitrary"),
                                  vmem_limit_bytes=VMEM_LIMIT)

    tile_a = lambda w: pl.BlockSpec((None, TM_A, w), lambda b, i: (b, i, 0))
    x1, qk, vt, cum = pl.pallas_call(
        _ffn1_qkv_body,
        grid=(bsz, seq // TM_A),
        in_specs=[tile_a(D_MODEL), _resident((1, D_MODEL)),
                  _resident((D_MODEL, D_FF)), _resident((D_MODEL, D_FF)),
                  _resident((D_FF, D_MODEL)), _resident((1, D_MODEL)),
                  _resident((D_MODEL, QKVF_COLS)), _resident((1, LANES))],
        out_specs=[tile_a(D_MODEL), tile_a(2 * ATTN_W),
                   pl.BlockSpec((None, N_HEADS * VT_ROWS, TM_A), lambda b, i: (b, 0, i)),
                   tile_a(LANES)],
        out_shape=[jax.ShapeDtypeStruct((bsz, seq, D_MODEL), F32),
                   jax.ShapeDtypeStruct((bsz, seq, 2 * ATTN_W), BF16),
                   jax.ShapeDtypeStruct((bsz, N_HEADS * VT_ROWS, seq), BF16),
                   jax.ShapeDtypeStruct((bsz, seq, LANES), F32)],
        scratch_shapes=[pltpu.VMEM((TM_A, D_MODEL), BF16), pltpu.VMEM((TM_A, D_FF), BF16),
                        pltpu.VMEM((1, LANES), F32)],
        compiler_params=params,
        name="ffn1_qkv",
    )(x, row(ffn1_norm), ffn1_gate.astype(BF16), ffn1_up.astype(BF16),
      ffn1_down.astype(BF16), row(mix_norm), w_a, b_f)

    y = pl.pallas_call(
        _fox_attn_body,
        grid=(bsz, seq // TQ),
        in_specs=[pl.BlockSpec((None, TQ, ATTN_W), lambda b, i: (b, i, 0)),
                  pl.BlockSpec((None, seq, ATTN_W), lambda b, i: (b, 0, 1)),
                  pl.BlockSpec((None, N_HEADS * VT_ROWS, seq), lambda b, i: (b, 0, 0)),
                  pl.BlockSpec((None, seq, LANES), lambda b, i: (b, 0, 0)),
                  _resident((N_SPLIT * LANES, ATTN_W)), _resident((N_SPLIT * LANES, ATTN_W))],
        out_specs=pl.BlockSpec((None, TQ, ATTN_W), lambda b, i: (b, i, 0)),
        out_shape=jax.ShapeDtypeStruct((bsz, seq, ATTN_W), BF16),
        scratch_shapes=[pltpu.VMEM((N_HEADS // 2, seq // TK, 2 * TK, 2 * LANES), BF16),
                        pltpu.VMEM((N_HEADS // 2, TQ, 2 * LANES), BF16),
                        pltpu.VMEM((N_HEADS // 2, 2 * TK, TQ), F32),
                        pltpu.VMEM((N_HEADS, TQ), F32),
                        pltpu.VMEM((N_HEADS * VT_ROWS, TQ), F32),
                        pltpu.VMEM((ATTN_W, TQ), F32)],
        compiler_params=params,
        name="fox_attn",
    )(qk, qk, vt, cum, sel_q, sel_k)

    tile_c = lambda w: pl.BlockSpec((None, TM_C, w), lambda b, i: (b, i, 0))
    return pl.pallas_call(
        _mix_ffn2_body,
        grid=(bsz, seq // TM_C),
        in_specs=[tile_c(D_MODEL), tile_c(ATTN_W), _resident((1, D_MODEL)),
                  _resident((D_MODEL, GATE_COLS)), _resident((3, CONV_W)),
                  _resident((ATTN_W, D_MODEL)), _resident((CONV_W, D_MODEL)),
                  _resident((D_MODEL, D_MODEL)), _resident((1, D_MODEL)),
                  _resident((D_MODEL, D_FF)), _resident((D_MODEL, D_FF)),
                  _resident((D_FF, D_MODEL)), _resident((1, D_MODEL))],
        out_specs=tile_c(D_MODEL),
        out_shape=jax.ShapeDtypeStruct((bsz, seq, D_MODEL), F32),
        scratch_shapes=[pltpu.VMEM((TM_C, D_MODEL), BF16), pltpu.VMEM((TM_C, D_FF), BF16),
                        pltpu.VMEM((2, CONV_W), F32)],
        compiler_params=params,
        name="mix_ffn2",
    )(x1, y, row(mix_norm), w_gate, conv_w.astype(F32), w_o_attn.astype(BF16),
      w_o_conv.astype(BF16), w_out.astype(BF16), row(ffn2_norm), ffn2_gate.astype(BF16),
      ffn2_up.astype(BF16), ffn2_down.astype(BF16), row(final_norm))
```

```python
import math

import numpy as np
import jax
import jax.numpy as jnp
from jax import lax
from jax.experimental import pallas as pl
from jax.experimental.pallas import tpu as pltpu

F32 = jnp.float32
BF16 = jnp.bfloat16

D_MODEL = 1024
N_HEADS = 8
HEAD_DIM = 64
ATTN_W = N_HEADS * HEAD_DIM
CONV_W = D_MODEL // 2
D_FF = 2816
RMS_EPS = 1e-6
FFN_RES = 0.5
LANES = 128
QKVF_COLS = 3 * ATTN_W + LANES
GATE_COLS = 3 * CONV_W + 2 * D_MODEL
FF_CHUNK = 256
TM_A = 512
TM_C = 512
SUB_A = 1
SUB_C = 2
TQ = 256
TK = TQ
SCORES_AHEAD = 3
LOG2E = 1.4426950408889634
NEG_BIG = -1e30
VT_ROWS = HEAD_DIM + 16
N_SPLIT = 3
BIAS_SLOT = 8
ONES_LANE = N_HEADS
VMEM_LIMIT = 56 * 1024 * 1024


def _rms(x, g):
    inv = lax.rsqrt(jnp.mean(x * x, axis=-1, keepdims=True) + RMS_EPS)
    return (x * inv) * g


def _sub_rows(tm, n_sub):
    th = tm // n_sub
    return [slice(i * th, (i + 1) * th) for i in range(n_sub)]


def _swiglu_act(h_scr, act_scr, wg_ref, wu_ref, subs, fill_h=None):
    for c in range(D_FF // FF_CHUNK):
        sl = slice(c * FF_CHUNK, (c + 1) * FF_CHUNK)
        for rows in subs:
            if c == 0 and fill_h is not None:
                fill_h(rows)
            g = jnp.dot(h_scr[rows, :], wg_ref[:, sl], preferred_element_type=F32)
            u = jnp.dot(h_scr[rows, :], wu_ref[:, sl], preferred_element_type=F32)
            act_scr[rows, sl] = (g * jax.nn.sigmoid(g) * u).astype(BF16)


def _cumsum_rows(x):
    n = x.shape[0]
    row = lax.broadcasted_iota(jnp.int32, x.shape, 0)
    d = 1
    while d < n:
        x = x + jnp.where(row >= d, pltpu.roll(x, d, axis=0), 0.0)
        d *= 2
    return x


def _ffn1_qkv_body(x_ref, g1_ref, wg_ref, wu_ref, wd_ref, g2_ref, wa_ref, bf_ref,
                   x1_ref, qk_ref, vt_ref, cc_ref,
                   h_scr, act_scr, carry_scr):
    subs = _sub_rows(TM_A, SUB_A)

    def fill_h(rows):
        h_scr[rows, :] = _rms(x_ref[rows, :], g1_ref[...]).astype(BF16)
    _swiglu_act(h_scr, act_scr, wg_ref, wu_ref, subs, fill_h)
    for rows in subs:
        x1 = x_ref[rows, :] + FFN_RES * jnp.dot(act_scr[rows, :], wd_ref[...],
                                                preferred_element_type=F32)
        x1_ref[rows, :] = x1
        h_scr[rows, :] = _rms(x1, g2_ref[...]).astype(BF16)

    @pl.when(pl.program_id(1) == 0)
    def _():
        carry_scr[...] = jnp.zeros_like(carry_scr)

    carry = carry_scr[...]
    for rows in subs:
        pr = jnp.dot(h_scr[rows, :], wa_ref[...], preferred_element_type=F32)
        qk_ref[rows, :ATTN_W] = (pr[:, :ATTN_W] * (LOG2E / math.sqrt(HEAD_DIM))).astype(BF16)
        qk_ref[rows, ATTN_W:] = pr[:, ATTN_W:2 * ATTN_W].astype(BF16)
        v_t = pr[:, 2 * ATTN_W:3 * ATTN_W].T.astype(BF16)
        for h in range(N_HEADS):
            vt_ref[h * VT_ROWS:h * VT_ROWS + HEAD_DIM, rows] = v_t[h * HEAD_DIM:(h + 1) * HEAD_DIM, :]
            vt_ref[h * VT_ROWS + HEAD_DIM:(h + 1) * VT_ROWS, rows] = jnp.ones(
                (VT_ROWS - HEAD_DIM, v_t.shape[1]), BF16)
        z = pr[:, 3 * ATTN_W:] + bf_ref[...]
        log_f = jnp.minimum(z, 0.0) - jnp.log1p(jnp.exp(-jnp.abs(z)))
        cum = _cumsum_rows(log_f) + carry
        carry = cum[cum.shape[0] - 1:, :]
        cc_ref[rows, :] = cum * LOG2E
    carry_scr[...] = carry


def _split_cum(c):
    pieces = []
    r = c
    for _ in range(N_SPLIT):
        p = r.astype(BF16)
        pieces.append(p)
        r = r - p.astype(F32)
    lane = lax.broadcasted_iota(jnp.int32, c.shape, 1)
    pieces[0] = jnp.where(lane == ONES_LANE, jnp.ones_like(pieces[0]), pieces[0])
    return jnp.concatenate(pieces, axis=1)


def _bias_selectors():
    sq = np.zeros((N_SPLIT * LANES, ATTN_W), np.float32)
    sk = np.zeros((N_SPLIT * LANES, ATTN_W), np.float32)
    for h in range(N_HEADS):
        base = (h // 2) * LANES + (h % 2) * BIAS_SLOT
        for s in range(N_SPLIT):
            sq[s * LANES + h, base + s] = 1.0
            sq[ONES_LANE, base + N_SPLIT + s] = 1.0
            sk[ONES_LANE, base + s] = 1.0
            sk[s * LANES + h, base + N_SPLIT + s] = -1.0
    return jnp.asarray(sq, BF16), jnp.asarray(sk, BF16)


def _fox_attn_body(qk_ref, vt_ref, cc_ref, sq_ref, sk_ref, y_ref,
                   kf_scr, qc_scr, st_scr, m_scr, ot_scr, yt_scr):
    n_blk = qk_ref.shape[0] // TQ
    n_pairs = N_HEADS // 2
    lane = lax.broadcasted_iota(jnp.int32, (TK, LANES), 1)

    for r in range(n_blk):
        rows_r = slice(r * TK, (r + 1) * TK)
        pieces = _split_cum(cc_ref[rows_r, :])
        ka = jnp.dot(pieces, sk_ref[...], preferred_element_type=F32).astype(BF16)
        qa = jnp.dot(pieces, sq_ref[...], preferred_element_type=F32).astype(BF16)
        for pair in range(n_pairs):
            cols = slice(pair * LANES, (pair + 1) * LANES)
            kp = qk_ref[rows_r, ATTN_W + pair * LANES:ATTN_W + (pair + 1) * LANES]
            kap = ka[:, cols]
            for hh in range(2):
                own = (lane >= hh * HEAD_DIM) & (lane < (hh + 1) * HEAD_DIM)
                own_b = (lane >= hh * BIAS_SLOT) & (lane < (hh + 1) * BIAS_SLOT)
                rows = slice(hh * TK, (hh + 1) * TK)
                kf_scr[pair, r, rows, :LANES] = jnp.where(own, kp, jnp.zeros_like(kp))
                kf_scr[pair, r, rows, LANES:] = jnp.where(own_b, kap, jnp.zeros_like(kap))
            qc_scr[r, pair, :, :LANES] = qk_ref[rows_r, cols]
            qc_scr[r, pair, :, LANES:] = qa[:, cols]

    k_idx = lax.broadcasted_iota(jnp.int32, (TK, TQ), 0)
    q_idx = lax.broadcasted_iota(jnp.int32, (TK, TQ), 1)

    stream = [(qi, j, pair) for qi in range(n_blk) for j in range(qi + 1)
              for pair in range(n_pairs)]

    def scores(n):
        qi, j, pair = stream[n]
        st_scr[pair] = lax.dot_general(kf_scr[pair, j], qc_scr[qi, pair],
                                       (((1,), (1,)), ((), ())), preferred_element_type=F32)

    def head(par, h, j, on_diagonal):
        st = st_scr[h // 2, (h % 2) * TK:(h % 2 + 1) * TK, :]
        if on_diagonal:
            st = jnp.where(q_idx >= k_idx, st, NEG_BIG)
        m_old = m_scr[par, h:h + 1, :]
        m_new = jnp.maximum(m_old, jnp.max(st, axis=0, keepdims=True))
        alpha = jnp.exp2(m_old - m_new)
        p = jnp.exp2(st - m_new).astype(BF16)
        m_scr[par, h:h + 1, :] = m_new
        rows = slice(h * VT_ROWS, (h + 1) * VT_ROWS)
        ot_scr[par, rows, :] = alpha * ot_scr[par, rows, :] + jnp.dot(
            vt_ref[rows, j * TK:(j + 1) * TK], p, preferred_element_type=F32)

    for n in range(SCORES_AHEAD):
        scores(n)
    for n, (qi, j, pair) in enumerate(stream):
        par = qi % 2
        if n + SCORES_AHEAD < len(stream):
            scores(n + SCORES_AHEAD)
        if j == 0 and pair == 0:
            m_scr[par] = jnp.full(m_scr.shape[1:], NEG_BIG, F32)
            ot_scr[par] = jnp.zeros(ot_scr.shape[1:], F32)
        head(par, 2 * pair, j, j == qi)
        head(par, 2 * pair + 1, j, j == qi)
        if j == qi and pair == n_pairs - 1:
            for h in range(N_HEADS):
                o = ot_scr[par, h * VT_ROWS:h * VT_ROWS + HEAD_DIM, :]
                l = ot_scr[par, h * VT_ROWS + HEAD_DIM:h * VT_ROWS + HEAD_DIM + 1, :]
                yt_scr[par, h * HEAD_DIM:(h + 1) * HEAD_DIM, :] = o / l
            y_ref[qi * TQ:(qi + 1) * TQ, :] = yt_scr[par].T.astype(BF16)


def _mix_ffn2_body(x1_ref, y_ref, gm_ref, wgate_ref, cw_ref, woa_ref, woc_ref, wout_ref,
                   g3_ref, wg_ref, wu_ref, wd_ref, gf_ref,
                   o_ref, h_scr, act_scr, tail_scr):
    subs = _sub_rows(TM_C, SUB_C)
    th = TM_C // SUB_C

    def gate(rows, lo, hi):
        return jnp.dot(h_scr[rows, :], wgate_ref[:, lo:hi], preferred_element_type=F32)

    @pl.when(pl.program_id(1) == 0)
    def _():
        tail_scr[...] = jnp.zeros_like(tail_scr)

    row = lax.broadcasted_iota(jnp.int32, (th, CONV_W), 0)
    tail = tail_scr[...]
    mixed = []
    for rows in subs:
        h_scr[rows, :] = _rms(x1_ref[rows, :], gm_ref[...]).astype(BF16)
        c_b = gate(rows, 0, CONV_W)
        u = gate(rows, CONV_W, 2 * CONV_W) * gate(rows, 2 * CONV_W, 3 * CONV_W)
        prev2, prev1 = tail[0:1, :], tail[1:2, :]
        u1 = jnp.where(row == 0, prev1, pltpu.roll(u, 1, axis=0))
        u2 = jnp.where(row == 0, prev2, jnp.where(row == 1, prev1, pltpu.roll(u, 2, axis=0)))
        tail = u[th - 2:th, :]
        conv = cw_ref[0:1, :] * u2 + cw_ref[1:2, :] * u1 + cw_ref[2:3, :] * u
        mixed.append((c_b * conv).astype(BF16))
    tail_scr[...] = tail

    o0 = 3 * CONV_W
    for rows, z in zip(subs, mixed):
        y_conv = jnp.dot(z, woc_ref[...], preferred_element_type=F32)
        y_attn = jnp.dot(y_ref[rows, :], woa_ref[...], preferred_element_type=F32)
        merged = (jax.nn.sigmoid(gate(rows, o0, o0 + D_MODEL)) * y_attn
                  + jax.nn.sigmoid(gate(rows, o0 + D_MODEL, o0 + 2 * D_MODEL)) * y_conv)
        act_scr[rows, :D_MODEL] = merged.astype(BF16)
    for rows in subs:
        x2 = x1_ref[rows, :] + jnp.dot(act_scr[rows, :D_MODEL], wout_ref[...],
                                       preferred_element_type=F32)
        o_ref[rows, :] = x2
        h_scr[rows, :] = _rms(x2, g3_ref[...]).astype(BF16)

    _swiglu_act(h_scr, act_scr, wg_ref, wu_ref, subs)
    for rows in subs:
        x3 = o_ref[rows, :] + FFN_RES * jnp.dot(act_scr[rows, :], wd_ref[...],
                                                preferred_element_type=F32)
        o_ref[rows, :] = _rms(x3, gf_ref[...])


def _resident(shape):
    return pl.BlockSpec(shape, lambda *_: (0,) * len(shape), pipeline_mode=pl.Buffered(1))


def kernel(x, ffn1_norm, ffn1_gate, ffn1_up, ffn1_down, mix_norm, w_in, b_forget, conv_w,
           w_o_attn, w_o_conv, w_out, ffn2_norm, ffn2_gate, ffn2_up, ffn2_down, final_norm):
    bsz, seq, d = x.shape
    assert d == D_MODEL and seq % TM_A == 0 and seq % TM_C == 0 and seq % TQ == 0
    row = lambda v: v.reshape(1, -1).astype(F32)
    n_qkvf = 3 * ATTN_W + N_HEADS
    w_a = jnp.pad(w_in[:, :n_qkvf], ((0, 0), (0, QKVF_COLS - n_qkvf))).astype(BF16)
    w_gate = w_in[:, n_qkvf:].astype(BF16)
    b_f = jnp.pad(row(b_forget), ((0, 0), (0, LANES - N_HEADS)))
    sel_q, sel_k = _bias_selectors()
    params = pltpu.CompilerParams(dimension_semantics=("arbitrary", "arbitrary"),
                                  vmem_limit_bytes=VMEM_LIMIT)

    tile_a = lambda w: pl.BlockSpec((None, TM_A, w), lambda b, i: (b, i, 0))
    x1, qk, vt, cum = pl.pallas_call(
        _ffn1_qkv_body,
        grid=(bsz, seq // TM_A),
        in_specs=[tile_a(D_MODEL), _resident((1, D_MODEL)),
                  _resident((D_MODEL, D_FF)), _resident((D_MODEL, D_FF)),
                  _resident((D_FF, D_MODEL)), _resident((1, D_MODEL)),
                  _resident((D_MODEL, QKVF_COLS)), _resident((1, LANES))],
        out_specs=[tile_a(D_MODEL), tile_a(2 * ATTN_W),
                   pl.BlockSpec((None, N_HEADS * VT_ROWS, TM_A), lambda b, i: (b, 0, i)),
                   tile_a(LANES)],
        out_shape=[jax.ShapeDtypeStruct((bsz, seq, D_MODEL), F32),
                   jax.ShapeDtypeStruct((bsz, seq, 2 * ATTN_W), BF16),
                   jax.ShapeDtypeStruct((bsz, N_HEADS * VT_ROWS, seq), BF16),
                   jax.ShapeDtypeStruct((bsz, seq, LANES), F32)],
        scratch_shapes=[pltpu.VMEM((TM_A, D_MODEL), BF16), pltpu.VMEM((TM_A, D_FF), BF16),
                        pltpu.VMEM((1, LANES), F32)],
        compiler_params=params,
        name="ffn1_qkv",
    )(x, row(ffn1_norm), ffn1_gate.astype(BF16), ffn1_up.astype(BF16),
      ffn1_down.astype(BF16), row(mix_norm), w_a, b_f)

    n_blk = seq // TQ
    y = pl.pallas_call(
        _fox_attn_body,
        grid=(bsz,),
        in_specs=[pl.BlockSpec((None, seq, 2 * ATTN_W), lambda b: (b, 0, 0)),
                  pl.BlockSpec((None, N_HEADS * VT_ROWS, seq), lambda b: (b, 0, 0)),
                  pl.BlockSpec((None, seq, LANES), lambda b: (b, 0, 0)),
                  _resident((N_SPLIT * LANES, ATTN_W)), _resident((N_SPLIT * LANES, ATTN_W))],
        out_specs=pl.BlockSpec((None, seq, ATTN_W), lambda b: (b, 0, 0)),
        out_shape=jax.ShapeDtypeStruct((bsz, seq, ATTN_W), BF16),
        scratch_shapes=[pltpu.VMEM((N_HEADS // 2, n_blk, 2 * TK, 2 * LANES), BF16),
                        pltpu.VMEM((n_blk, N_HEADS // 2, TQ, 2 * LANES), BF16),
                        pltpu.VMEM((N_HEADS // 2, 2 * TK, TQ), F32),
                        pltpu.VMEM((2, N_HEADS, TQ), F32),
                        pltpu.VMEM((2, N_HEADS * VT_ROWS, TQ), F32),
                        pltpu.VMEM((2, ATTN_W, TQ), F32)],
        compiler_params=pltpu.CompilerParams(dimension_semantics=("arbitrary",),
                                             vmem_limit_bytes=VMEM_LIMIT),
        name="fox_attn",
    )(qk, vt, cum, sel_q, sel_k)

    tile_c = lambda w: pl.BlockSpec((None, TM_C, w), lambda b, i: (b, i, 0))
    return pl.pallas_call(
        _mix_ffn2_body,
        grid=(bsz, seq // TM_C),
        in_specs=[tile_c(D_MODEL), tile_c(ATTN_W), _resident((1, D_MODEL)),
                  _resident((D_MODEL, GATE_COLS)), _resident((3, CONV_W)),
                  _resident((ATTN_W, D_MODEL)), _resident((CONV_W, D_MODEL)),
                  _resident((D_MODEL, D_MODEL)), _resident((1, D_MODEL)),
                  _resident((D_MODEL, D_FF)), _resident((D_MODEL, D_FF)),
                  _resident((D_FF, D_MODEL)), _resident((1, D_MODEL))],
        out_specs=tile_c(D_MODEL),
        out_shape=jax.ShapeDtypeStruct((bsz, seq, D_MODEL), F32),
        scratch_shapes=[pltpu.VMEM((TM_C, D_MODEL), BF16), pltpu.VMEM((TM_C, D_FF), BF16),
                        pltpu.VMEM((2, CONV_W), F32)],
        compiler_params=params,
        name="mix_ffn2",
    )(x1, y, row(mix_norm), w_gate, conv_w.astype(F32), w_o_attn.astype(BF16),
      w_o_conv.astype(BF16), w_out.astype(BF16), row(ffn2_norm), ffn2_gate.astype(BF16),
      ffn2_up.astype(BF16), ffn2_down.astype(BF16), row(final_norm))
```

```python
import math

import numpy as np
import jax
import jax.numpy as jnp
from jax import lax
from jax.experimental import pallas as pl
from jax.experimental.pallas import tpu as pltpu

F32 = jnp.float32
BF16 = jnp.bfloat16

D_MODEL = 1024
N_HEADS = 8
HEAD_DIM = 64
ATTN_W = N_HEADS * HEAD_DIM
CONV_W = D_MODEL // 2
D_FF = 2816
RMS_EPS = 1e-6
FFN_RES = 0.5
LANES = 128
QKVF_COLS = 3 * ATTN_W + LANES
GATE_COLS = 3 * CONV_W + 2 * D_MODEL
FF_CHUNK = 256
TM_A = 1024
TM_C = 512
SUB_A = 2
SUB_C = 2
TQ = 256
TK = TQ
SCORES_AHEAD = 3
LOG2E = 1.4426950408889634
NEG_BIG = -1e30
VT_ROWS = HEAD_DIM + 16
N_SPLIT = 3
BIAS_SLOT = 8
ONES_LANE = N_HEADS
VMEM_LIMIT = 56 * 1024 * 1024


def _rms(x, g):
    inv = lax.rsqrt(jnp.mean(x * x, axis=-1, keepdims=True) + RMS_EPS)
    return (x * inv) * g


def _sub_rows(tm, n_sub):
    th = tm // n_sub
    return [slice(i * th, (i + 1) * th) for i in range(n_sub)]


def _swiglu_act(h_scr, act_scr, wg_ref, wu_ref, subs, fill_h=None):
    for c in range(D_FF // FF_CHUNK):
        sl = slice(c * FF_CHUNK, (c + 1) * FF_CHUNK)
        for rows in subs:
            if c == 0 and fill_h is not None:
                fill_h(rows)
            g = jnp.dot(h_scr[rows, :], wg_ref[:, sl], preferred_element_type=F32)
            u = jnp.dot(h_scr[rows, :], wu_ref[:, sl], preferred_element_type=F32)
            act_scr[rows, sl] = (g * jax.nn.sigmoid(g) * u).astype(BF16)


def _cumsum_rows(x):
    n = x.shape[0]
    row = lax.broadcasted_iota(jnp.int32, x.shape, 0)
    d = 1
    while d < n:
        x = x + jnp.where(row >= d, pltpu.roll(x, d, axis=0), 0.0)
        d *= 2
    return x


def _ffn1_qkv_body(x_ref, g1_ref, wg_ref, wu_ref, wd_ref, g2_ref, wa_ref, bf_ref,
                   x1_ref, qk_ref, vt_ref, cc_ref,
                   h_scr, act_scr, carry_scr):
    subs = _sub_rows(TM_A, SUB_A)

    def fill_h(rows):
        h_scr[rows, :] = _rms(x_ref[rows, :], g1_ref[...]).astype(BF16)
    _swiglu_act(h_scr, act_scr, wg_ref, wu_ref, subs, fill_h)
    for rows in subs:
        x1 = x_ref[rows, :] + FFN_RES * jnp.dot(act_scr[rows, :], wd_ref[...],
                                                preferred_element_type=F32)
        x1_ref[rows, :] = x1
        h_scr[rows, :] = _rms(x1, g2_ref[...]).astype(BF16)

    @pl.when(pl.program_id(1) == 0)
    def _():
        carry_scr[...] = jnp.zeros_like(carry_scr)

    carry = carry_scr[...]
    for rows in subs:
        pr = jnp.dot(h_scr[rows, :], wa_ref[...], preferred_element_type=F32)
        qk_ref[rows, :ATTN_W] = (pr[:, :ATTN_W] * (LOG2E / math.sqrt(HEAD_DIM))).astype(BF16)
        qk_ref[rows, ATTN_W:] = pr[:, ATTN_W:2 * ATTN_W].astype(BF16)
        v_t = pr[:, 2 * ATTN_W:3 * ATTN_W].T.astype(BF16)
        for h in range(N_HEADS):
            vt_ref[h * VT_ROWS:h * VT_ROWS + HEAD_DIM, rows] = v_t[h * HEAD_DIM:(h + 1) * HEAD_DIM, :]
            vt_ref[h * VT_ROWS + HEAD_DIM:(h + 1) * VT_ROWS, rows] = jnp.ones(
                (VT_ROWS - HEAD_DIM, v_t.shape[1]), BF16)
        z = pr[:, 3 * ATTN_W:] + bf_ref[...]
        log_f = jnp.minimum(z, 0.0) - jnp.log1p(jnp.exp(-jnp.abs(z)))
        cum = _cumsum_rows(log_f) + carry
        carry = cum[cum.shape[0] - 1:, :]
        cc_ref[rows, :] = cum * LOG2E
    carry_scr[...] = carry


def _split_cum(c):
    pieces = []
    r = c
    for _ in range(N_SPLIT):
        p = r.astype(BF16)
        pieces.append(p)
        r = r - p.astype(F32)
    lane = lax.broadcasted_iota(jnp.int32, c.shape, 1)
    pieces[0] = jnp.where(lane == ONES_LANE, jnp.ones_like(pieces[0]), pieces[0])
    return jnp.concatenate(pieces, axis=1)


def _bias_selectors():
    sq = np.zeros((N_SPLIT * LANES, ATTN_W), np.float32)
    sk = np.zeros((N_SPLIT * LANES, ATTN_W), np.float32)
    for h in range(N_HEADS):
        base = (h // 2) * LANES + (h % 2) * BIAS_SLOT
        for s in range(N_SPLIT):
            sq[s * LANES + h, base + s] = 1.0
            sq[ONES_LANE, base + N_SPLIT + s] = 1.0
            sk[ONES_LANE, base + s] = 1.0
            sk[s * LANES + h, base + N_SPLIT + s] = -1.0
    return jnp.asarray(sq, BF16), jnp.asarray(sk, BF16)


def _fox_attn_body(qk_ref, vt_ref, cc_ref, sq_ref, sk_ref, y_ref,
                   kf_scr, qc_scr, st_scr, m_scr, ot_scr, yt_scr):
    n_blk = qk_ref.shape[0] // TQ
    n_pairs = N_HEADS // 2
    lane = lax.broadcasted_iota(jnp.int32, (TK, LANES), 1)

    for r in range(n_blk):
        rows_r = slice(r * TK, (r + 1) * TK)
        pieces = _split_cum(cc_ref[rows_r, :])
        ka = jnp.dot(pieces, sk_ref[...], preferred_element_type=F32).astype(BF16)
        qa = jnp.dot(pieces, sq_ref[...], preferred_element_type=F32).astype(BF16)
        for pair in range(n_pairs):
            cols = slice(pair * LANES, (pair + 1) * LANES)
            kp = qk_ref[rows_r, ATTN_W + pair * LANES:ATTN_W + (pair + 1) * LANES]
            kap = ka[:, cols]
            for hh in range(2):
                own = (lane >= hh * HEAD_DIM) & (lane < (hh + 1) * HEAD_DIM)
                own_b = (lane >= hh * BIAS_SLOT) & (lane < (hh + 1) * BIAS_SLOT)
                rows = slice(hh * TK, (hh + 1) * TK)
                kf_scr[pair, r, rows, :LANES] = jnp.where(own, kp, jnp.zeros_like(kp))
                kf_scr[pair, r, rows, LANES:] = jnp.where(own_b, kap, jnp.zeros_like(kap))
            qc_scr[r, pair, :, :LANES] = qk_ref[rows_r, cols]
            qc_scr[r, pair, :, LANES:] = qa[:, cols]

    k_idx = lax.broadcasted_iota(jnp.int32, (TK, TQ), 0)
    q_idx = lax.broadcasted_iota(jnp.int32, (TK, TQ), 1)

    stream = [(qi, j, pair) for qi in range(n_blk) for j in range(qi + 1)
              for pair in range(n_pairs)]

    def scores(n):
        qi, j, pair = stream[n]
        st_scr[pair] = lax.dot_general(kf_scr[pair, j], qc_scr[qi, pair],
                                       (((1,), (1,)), ((), ())), preferred_element_type=F32)

    def head(par, h, j, on_diagonal):
        st = st_scr[h // 2, (h % 2) * TK:(h % 2 + 1) * TK, :]
        if on_diagonal:
            st = jnp.where(q_idx >= k_idx, st, NEG_BIG)
        m_old = m_scr[par, h:h + 1, :]
        m_new = jnp.maximum(m_old, jnp.max(st, axis=0, keepdims=True))
        alpha = jnp.exp2(m_old - m_new)
        p = jnp.exp2(st - m_new).astype(BF16)
        m_scr[par, h:h + 1, :] = m_new
        rows = slice(h * VT_ROWS, (h + 1) * VT_ROWS)
        ot_scr[par, rows, :] = alpha * ot_scr[par, rows, :] + jnp.dot(
            vt_ref[rows, j * TK:(j + 1) * TK], p, preferred_element_type=F32)

    for n in range(SCORES_AHEAD):
        scores(n)
    for n, (qi, j, pair) in enumerate(stream):
        par = qi % 2
        if n + SCORES_AHEAD < len(stream):
            scores(n + SCORES_AHEAD)
        if j == 0 and pair == 0:
            m_scr[par] = jnp.full(m_scr.shape[1:], NEG_BIG, F32)
            ot_scr[par] = jnp.zeros(ot_scr.shape[1:], F32)
        head(par, 2 * pair, j, j == qi)
        head(par, 2 * pair + 1, j, j == qi)
        if j == qi and pair == n_pairs - 1:
            for h in range(N_HEADS):
                o = ot_scr[par, h * VT_ROWS:h * VT_ROWS + HEAD_DIM, :]
                l = ot_scr[par, h * VT_ROWS + HEAD_DIM:h * VT_ROWS + HEAD_DIM + 1, :]
                yt_scr[par, h * HEAD_DIM:(h + 1) * HEAD_DIM, :] = o / l
            y_ref[qi * TQ:(qi + 1) * TQ, :] = yt_scr[par].T.astype(BF16)


def _mix_ffn2_body(x1_ref, y_ref, gm_ref, wgate_ref, cw_ref, woa_ref, woc_ref, wout_ref,
                   g3_ref, wg_ref, wu_ref, wd_ref, gf_ref,
                   o_ref, h_scr, act_scr, tail_scr):
    subs = _sub_rows(TM_C, SUB_C)
    th = TM_C // SUB_C

    def gate(rows, lo, hi):
        return jnp.dot(h_scr[rows, :], wgate_ref[:, lo:hi], preferred_element_type=F32)

    @pl.when(pl.program_id(1) == 0)
    def _():
        tail_scr[...] = jnp.zeros_like(tail_scr)

    row = lax.broadcasted_iota(jnp.int32, (th, CONV_W), 0)
    tail = tail_scr[...]
    mixed = []
    for rows in subs:
        h_scr[rows, :] = _rms(x1_ref[rows, :], gm_ref[...]).astype(BF16)
        c_b = gate(rows, 0, CONV_W)
        u = gate(rows, CONV_W, 2 * CONV_W) * gate(rows, 2 * CONV_W, 3 * CONV_W)
        prev2, prev1 = tail[0:1, :], tail[1:2, :]
        u1 = jnp.where(row == 0, prev1, pltpu.roll(u, 1, axis=0))
        u2 = jnp.where(row == 0, prev2, jnp.where(row == 1, prev1, pltpu.roll(u, 2, axis=0)))
        tail = u[th - 2:th, :]
        conv = cw_ref[0:1, :] * u2 + cw_ref[1:2, :] * u1 + cw_ref[2:3, :] * u
        mixed.append((c_b * conv).astype(BF16))
    tail_scr[...] = tail

    o0 = 3 * CONV_W
    for rows, z in zip(subs, mixed):
        y_conv = jnp.dot(z, woc_ref[...], preferred_element_type=F32)
        y_attn = jnp.dot(y_ref[rows, :], woa_ref[...], preferred_element_type=F32)
        merged = (jax.nn.sigmoid(gate(rows, o0, o0 + D_MODEL)) * y_attn
                  + jax.nn.sigmoid(gate(rows, o0 + D_MODEL, o0 + 2 * D_MODEL)) * y_conv)
        act_scr[rows, :D_MODEL] = merged.astype(BF16)
    for rows in subs:
        x2 = x1_ref[rows, :] + jnp.dot(act_scr[rows, :D_MODEL], wout_ref[...],
                                       preferred_element_type=F32)
        o_ref[rows, :] = x2
        h_scr[rows, :] = _rms(x2, g3_ref[...]).astype(BF16)

    _swiglu_act(h_scr, act_scr, wg_ref, wu_ref, subs)
    for rows in subs:
        x3 = o_ref[rows, :] + FFN_RES * jnp.dot(act_scr[rows, :], wd_ref[...],
                                                preferred_element_type=F32)
        o_ref[rows, :] = _rms(x3, gf_ref[...])


def _resident(shape):
    return pl.BlockSpec(shape, lambda *_: (0,) * len(shape), pipeline_mode=pl.Buffered(1))


def kernel(x, ffn1_norm, ffn1_gate, ffn1_up, ffn1_down, mix_norm, w_in, b_forget, conv_w,
           w_o_attn, w_o_conv, w_out, ffn2_norm, ffn2_gate, ffn2_up, ffn2_down, final_norm):
    bsz, seq, d = x.shape
    assert d == D_MODEL and seq % TM_A == 0 and seq % TM_C == 0 and seq % TQ == 0
    row = lambda v: v.reshape(1, -1).astype(F32)
    n_qkvf = 3 * ATTN_W + N_HEADS
    w_in16 = w_in.astype(BF16)
    w_a = jnp.pad(w_in16[:, :n_qkvf], ((0, 0), (0, QKVF_COLS - n_qkvf)))
    w_gate = w_in16[:, n_qkvf:]
    b_f = jnp.pad(row(b_forget), ((0, 0), (0, LANES - N_HEADS)))
    sel_q, sel_k = _bias_selectors()
    params = pltpu.CompilerParams(dimension_semantics=("arbitrary", "arbitrary"),
                                  vmem_limit_bytes=VMEM_LIMIT)

    tile_a = lambda w: pl.BlockSpec((None, TM_A, w), lambda b, i: (b, i, 0))
    x1, qk, vt, cum = pl.pallas_call(
        _ffn1_qkv_body,
        grid=(bsz, seq // TM_A),
        in_specs=[tile_a(D_MODEL), _resident((1, D_MODEL)),
                  _resident((D_MODEL, D_FF)), _resident((D_MODEL, D_FF)),
                  _resident((D_FF, D_MODEL)), _resident((1, D_MODEL)),
                  _resident((D_MODEL, QKVF_COLS)), _resident((1, LANES))],
        out_specs=[tile_a(D_MODEL), tile_a(2 * ATTN_W),
                   pl.BlockSpec((None, N_HEADS * VT_ROWS, TM_A), lambda b, i: (b, 0, i)),
                   tile_a(LANES)],
        out_shape=[jax.ShapeDtypeStruct((bsz, seq, D_MODEL), F32),
                   jax.ShapeDtypeStruct((bsz, seq, 2 * ATTN_W), BF16),
                   jax.ShapeDtypeStruct((bsz, N_HEADS * VT_ROWS, seq), BF16),
                   jax.ShapeDtypeStruct((bsz, seq, LANES), F32)],
        scratch_shapes=[pltpu.VMEM((TM_A, D_MODEL), BF16), pltpu.VMEM((TM_A, D_FF), BF16),
                        pltpu.VMEM((1, LANES), F32)],
        compiler_params=params,
        name="ffn1_qkv",
    )(x, row(ffn1_norm), ffn1_gate.astype(BF16), ffn1_up.astype(BF16),
      ffn1_down.astype(BF16), row(mix_norm), w_a, b_f)

    n_blk = seq // TQ
    y = pl.pallas_call(
        _fox_attn_body,
        grid=(bsz,),
        in_specs=[pl.BlockSpec((None, seq, 2 * ATTN_W), lambda b: (b, 0, 0)),
                  pl.BlockSpec((None, N_HEADS * VT_ROWS, seq), lambda b: (b, 0, 0)),
                  pl.BlockSpec((None, seq, LANES), lambda b: (b, 0, 0)),
                  _resident((N_SPLIT * LANES, ATTN_W)), _resident((N_SPLIT * LANES, ATTN_W))],
        out_specs=pl.BlockSpec((None, seq, ATTN_W), lambda b: (b, 0, 0)),
        out_shape=jax.ShapeDtypeStruct((bsz, seq, ATTN_W), BF16),
        scratch_shapes=[pltpu.VMEM((N_HEADS // 2, n_blk, 2 * TK, 2 * LANES), BF16),
                        pltpu.VMEM((n_blk, N_HEADS // 2, TQ, 2 * LANES), BF16),
                        pltpu.VMEM((N_HEADS // 2, 2 * TK, TQ), F32),
                        pltpu.VMEM((2, N_HEADS, TQ), F32),
                        pltpu.VMEM((2, N_HEADS * VT_ROWS, TQ), F32),
                        pltpu.VMEM((2, ATTN_W, TQ), F32)],
        compiler_params=pltpu.CompilerParams(dimension_semantics=("arbitrary",),
                                             vmem_limit_bytes=VMEM_LIMIT),
        name="fox_attn",
    )(qk, vt, cum, sel_q, sel_k)

    tile_c = lambda w: pl.BlockSpec((None, TM_C, w), lambda b, i: (b, i, 0))
    return pl.pallas_call(
        _mix_ffn2_body,
        grid=(bsz, seq // TM_C),
        in_specs=[tile_c(D_MODEL), tile_c(ATTN_W), _resident((1, D_MODEL)),
                  _resident((D_MODEL, GATE_COLS)), _resident((3, CONV_W)),
                  _resident((ATTN_W, D_MODEL)), _resident((CONV_W, D_MODEL)),
                  _resident((D_MODEL, D_MODEL)), _resident((1, D_MODEL)),
                  _resident((D_MODEL, D_FF)), _resident((D_MODEL, D_FF)),
                  _resident((D_FF, D_MODEL)), _resident((1, D_MODEL))],
        out_specs=tile_c(D_MODEL),
        out_shape=jax.ShapeDtypeStruct((bsz, seq, D_MODEL), F32),
        scratch_shapes=[pltpu.VMEM((TM_C, D_MODEL), BF16), pltpu.VMEM((TM_C, D_FF), BF16),
                        pltpu.VMEM((2, CONV_W), F32)],
        compiler_params=params,
        name="mix_ffn2",
    )(x1, y, row(mix_norm), w_gate, conv_w.astype(F32), w_o_attn.astype(BF16),
      w_o_conv.astype(BF16), w_out.astype(BF16), row(ffn2_norm), ffn2_gate.astype(BF16),
      ffn2_up.astype(BF16), ffn2_down.astype(BF16), row(final_norm))
```

```python
import math

import numpy as np
import jax
import jax.numpy as jnp
from jax import lax
from jax.experimental import pallas as pl
from jax.experimental.pallas import tpu as pltpu

F32 = jnp.float32
BF16 = jnp.bfloat16

D_MODEL = 1024
N_HEADS = 8
HEAD_DIM = 64
ATTN_W = N_HEADS * HEAD_DIM
CONV_W = D_MODEL // 2
D_FF = 2816
RMS_EPS = 1e-6
FFN_RES = 0.5
LANES = 128
QKVF_COLS = 3 * ATTN_W + LANES
GATE_COLS = 3 * CONV_W + 2 * D_MODEL
FF_CHUNK = 256
TM_A = 1024
TM_C = 512
W_IN_ROWS = 256
SUB_A = 2
SUB_C = 2
TQ = 256
TK = TQ
SCORES_AHEAD = 3
ST_SLOTS = 8
LOG2E = 1.4426950408889634
NEG_BIG = -1e30
VT_ROWS = HEAD_DIM + 16
N_SPLIT = 3
BIAS_SLOT = 8
ONES_LANE = N_HEADS
VMEM_LIMIT = 56 * 1024 * 1024


def _rms(x, g):
    inv = lax.rsqrt(jnp.mean(x * x, axis=-1, keepdims=True) + RMS_EPS)
    return (x * inv) * g


def _sub_rows(tm, n_sub):
    th = tm // n_sub
    return [slice(i * th, (i + 1) * th) for i in range(n_sub)]


def _swiglu_act(h_scr, act_scr, wg_ref, wu_ref, subs, fill_h=None):
    for c in range(D_FF // FF_CHUNK):
        sl = slice(c * FF_CHUNK, (c + 1) * FF_CHUNK)
        for rows in subs:
            if c == 0 and fill_h is not None:
                fill_h(rows)
            g = jnp.dot(h_scr[rows, :], wg_ref[:, sl], preferred_element_type=F32)
            u = jnp.dot(h_scr[rows, :], wu_ref[:, sl], preferred_element_type=F32)
            act_scr[rows, sl] = (g * jax.nn.sigmoid(g) * u).astype(BF16)


def _cumsum_rows(x):
    n = x.shape[0]
    row = lax.broadcasted_iota(jnp.int32, x.shape, 0)
    d = 1
    while d < n:
        x = x + jnp.where(row >= d, pltpu.roll(x, d, axis=0), 0.0)
        d *= 2
    return x


def _ffn1_qkv_body(x_ref, g1_ref, wg_ref, wu_ref, wd_ref, g2_ref, wa_ref, bf_ref,
                   x1_ref, qk_ref, vt_ref, cc_ref,
                   h_scr, act_scr, carry_scr):
    subs = _sub_rows(TM_A, SUB_A)

    def fill_h(rows):
        h_scr[rows, :] = _rms(x_ref[rows, :], g1_ref[...]).astype(BF16)
    _swiglu_act(h_scr, act_scr, wg_ref, wu_ref, subs, fill_h)
    for rows in subs:
        x1 = x_ref[rows, :] + FFN_RES * jnp.dot(act_scr[rows, :], wd_ref[...],
                                                preferred_element_type=F32)
        x1_ref[rows, :] = x1
        h_scr[rows, :] = _rms(x1, g2_ref[...]).astype(BF16)

    @pl.when(pl.program_id(1) == 0)
    def _():
        carry_scr[...] = jnp.zeros_like(carry_scr)

    carry = carry_scr[...]
    for rows in subs:
        pr = jnp.dot(h_scr[rows, :], wa_ref[...], preferred_element_type=F32)
        qk_ref[rows, :ATTN_W] = (pr[:, :ATTN_W] * (LOG2E / math.sqrt(HEAD_DIM))).astype(BF16)
        qk_ref[rows, ATTN_W:] = pr[:, ATTN_W:2 * ATTN_W].astype(BF16)
        v_t = pr[:, 2 * ATTN_W:3 * ATTN_W].T.astype(BF16)
        for h in range(N_HEADS):
            vt_ref[h * VT_ROWS:h * VT_ROWS + HEAD_DIM, rows] = v_t[h * HEAD_DIM:(h + 1) * HEAD_DIM, :]
            vt_ref[h * VT_ROWS + HEAD_DIM:(h + 1) * VT_ROWS, rows] = jnp.ones(
                (VT_ROWS - HEAD_DIM, v_t.shape[1]), BF16)
        z = pr[:, 3 * ATTN_W:] + bf_ref[...]
        log_f = jnp.minimum(z, 0.0) - jnp.log1p(jnp.exp(-jnp.abs(z)))
        cum = _cumsum_rows(log_f) + carry
        carry = cum[cum.shape[0] - 1:, :]
        cc_ref[rows, :] = cum * LOG2E
    carry_scr[...] = carry


def _split_cum(c):
    pieces = []
    r = c
    for _ in range(N_SPLIT):
        p = r.astype(BF16)
        pieces.append(p)
        r = r - p.astype(F32)
    lane = lax.broadcasted_iota(jnp.int32, c.shape, 1)
    pieces[0] = jnp.where(lane == ONES_LANE, jnp.ones_like(pieces[0]), pieces[0])
    return jnp.concatenate(pieces, axis=1)


def _bias_selectors():
    sq = np.zeros((N_SPLIT * LANES, ATTN_W), np.float32)
    sk = np.zeros((N_SPLIT * LANES, ATTN_W), np.float32)
    for h in range(N_HEADS):
        base = (h // 2) * LANES + (h % 2) * BIAS_SLOT
        for s in range(N_SPLIT):
            sq[s * LANES + h, base + s] = 1.0
            sq[ONES_LANE, base + N_SPLIT + s] = 1.0
            sk[ONES_LANE, base + s] = 1.0
            sk[s * LANES + h, base + N_SPLIT + s] = -1.0
    return jnp.asarray(sq, BF16), jnp.asarray(sk, BF16)


def _fox_attn_body(qk_ref, vt_ref, cc_ref, sq_ref, sk_ref, y_ref,
                   kf_scr, qc_scr, st_scr, m_scr, ot_scr, yt_scr):
    n_blk = qk_ref.shape[0] // TQ
    n_pairs = N_HEADS // 2
    lane = lax.broadcasted_iota(jnp.int32, (TK, LANES), 1)

    for r in range(n_blk):
        rows_r = slice(r * TK, (r + 1) * TK)
        pieces = _split_cum(cc_ref[rows_r, :])
        ka = jnp.dot(pieces, sk_ref[...], preferred_element_type=F32).astype(BF16)
        qa = jnp.dot(pieces, sq_ref[...], preferred_element_type=F32).astype(BF16)
        for pair in range(n_pairs):
            cols = slice(pair * LANES, (pair + 1) * LANES)
            kp = qk_ref[rows_r, ATTN_W + pair * LANES:ATTN_W + (pair + 1) * LANES]
            kap = ka[:, cols]
            for hh in range(2):
                own = (lane >= hh * HEAD_DIM) & (lane < (hh + 1) * HEAD_DIM)
                own_b = (lane >= hh * BIAS_SLOT) & (lane < (hh + 1) * BIAS_SLOT)
                rows = slice(hh * TK, (hh + 1) * TK)
                kf_scr[pair, r, rows, :LANES] = jnp.where(own, kp, jnp.zeros_like(kp))
                kf_scr[pair, r, rows, LANES:] = jnp.where(own_b, kap, jnp.zeros_like(kap))
            qc_scr[r, pair, :, :LANES] = qk_ref[rows_r, cols]
            qc_scr[r, pair, :, LANES:] = qa[:, cols]

    k_idx = lax.broadcasted_iota(jnp.int32, (TK, TQ), 0)
    q_idx = lax.broadcasted_iota(jnp.int32, (TK, TQ), 1)

    stream = [(qi, j, pair) for qi in range(n_blk) for j in range(qi + 1)
              for pair in range(n_pairs)]

    def scores(n):
        qi, j, pair = stream[n]
        st_scr[n % ST_SLOTS] = lax.dot_general(kf_scr[pair, j], qc_scr[qi, pair],
                                               (((1,), (1,)), ((), ())),
                                               preferred_element_type=F32)

    def head(n, par, h, j, on_diagonal):
        st = st_scr[n % ST_SLOTS, (h % 2) * TK:(h % 2 + 1) * TK, :]
        if on_diagonal:
            st = jnp.where(q_idx >= k_idx, st, NEG_BIG)
        m_old = m_scr[par, h:h + 1, :]
        m_new = jnp.maximum(m_old, jnp.max(st, axis=0, keepdims=True))
        alpha = jnp.exp2(m_old - m_new)
        p = jnp.exp2(st - m_new).astype(BF16)
        m_scr[par, h:h + 1, :] = m_new
        rows = slice(h * VT_ROWS, (h + 1) * VT_ROWS)
        ot_scr[par, rows, :] = alpha * ot_scr[par, rows, :] + jnp.dot(
            vt_ref[rows, j * TK:(j + 1) * TK], p, preferred_element_type=F32)

    for n in range(SCORES_AHEAD):
        scores(n)
    for n, (qi, j, pair) in enumerate(stream):
        par = qi % 2
        if n + SCORES_AHEAD < len(stream):
            scores(n + SCORES_AHEAD)
        if j == 0 and pair == 0:
            m_scr[par] = jnp.full(m_scr.shape[1:], NEG_BIG, F32)
            ot_scr[par] = jnp.zeros(ot_scr.shape[1:], F32)
        head(n, par, 2 * pair, j, j == qi)
        head(n, par, 2 * pair + 1, j, j == qi)
        if j == qi and pair == n_pairs - 1:
            for h in range(N_HEADS):
                o = ot_scr[par, h * VT_ROWS:h * VT_ROWS + HEAD_DIM, :]
                l = ot_scr[par, h * VT_ROWS + HEAD_DIM:h * VT_ROWS + HEAD_DIM + 1, :]
                yt_scr[par, h * HEAD_DIM:(h + 1) * HEAD_DIM, :] = o / l
            y_ref[qi * TQ:(qi + 1) * TQ, :] = yt_scr[par].T.astype(BF16)


def _mix_ffn2_body(x1_ref, y_ref, gm_ref, wgate_ref, cw_ref, woa_ref, woc_ref, wout_ref,
                   g3_ref, wg_ref, wu_ref, wd_ref, gf_ref,
                   o_ref, h_scr, act_scr, tail_scr):
    subs = _sub_rows(TM_C, SUB_C)
    th = TM_C // SUB_C

    def gate(rows, lo, hi):
        return jnp.dot(h_scr[rows, :], wgate_ref[:, lo:hi], preferred_element_type=F32)

    @pl.when(pl.program_id(1) == 0)
    def _():
        tail_scr[...] = jnp.zeros_like(tail_scr)

    row = lax.broadcasted_iota(jnp.int32, (th, CONV_W), 0)
    tail = tail_scr[...]
    mixed = []
    for rows in subs:
        h_scr[rows, :] = _rms(x1_ref[rows, :], gm_ref[...]).astype(BF16)
        c_b = gate(rows, 0, CONV_W)
        u = gate(rows, CONV_W, 2 * CONV_W) * gate(rows, 2 * CONV_W, 3 * CONV_W)
        prev2, prev1 = tail[0:1, :], tail[1:2, :]
        u1 = jnp.where(row == 0, prev1, pltpu.roll(u, 1, axis=0))
        u2 = jnp.where(row == 0, prev2, jnp.where(row == 1, prev1, pltpu.roll(u, 2, axis=0)))
        tail = u[th - 2:th, :]
        conv = cw_ref[0:1, :] * u2 + cw_ref[1:2, :] * u1 + cw_ref[2:3, :] * u
        mixed.append((c_b * conv).astype(BF16))
    tail_scr[...] = tail

    o0 = 3 * CONV_W
    for rows, z in zip(subs, mixed):
        y_conv = jnp.dot(z, woc_ref[...], preferred_element_type=F32)
        y_attn = jnp.dot(y_ref[rows, :], woa_ref[...], preferred_element_type=F32)
        merged = (jax.nn.sigmoid(gate(rows, o0, o0 + D_MODEL)) * y_attn
                  + jax.nn.sigmoid(gate(rows, o0 + D_MODEL, o0 + 2 * D_MODEL)) * y_conv)
        act_scr[rows, :D_MODEL] = merged.astype(BF16)
    for rows in subs:
        x2 = x1_ref[rows, :] + jnp.dot(act_scr[rows, :D_MODEL], wout_ref[...],
                                       preferred_element_type=F32)
        o_ref[rows, :] = x2
        h_scr[rows, :] = _rms(x2, g3_ref[...]).astype(BF16)

    _swiglu_act(h_scr, act_scr, wg_ref, wu_ref, subs)
    for rows in subs:
        x3 = o_ref[rows, :] + FFN_RES * jnp.dot(act_scr[rows, :], wd_ref[...],
                                                preferred_element_type=F32)
        o_ref[rows, :] = _rms(x3, gf_ref[...])


def _split_w_in_body(w_ref, wa_ref, wg_ref):
    n_qkv = 3 * ATTN_W
    wa_ref[:, :n_qkv] = w_ref[:, :n_qkv].astype(BF16)
    tail = w_ref[:, n_qkv:n_qkv + LANES]
    lane = lax.broadcasted_iota(jnp.int32, tail.shape, 1)
    wa_ref[:, n_qkv:] = jnp.where(lane < N_HEADS, tail, 0.0).astype(BF16)
    wg_ref[...] = w_ref[:, n_qkv + N_HEADS:].astype(BF16)


def _resident(shape):
    return pl.BlockSpec(shape, lambda *_: (0,) * len(shape), pipeline_mode=pl.Buffered(1))


def kernel(x, ffn1_norm, ffn1_gate, ffn1_up, ffn1_down, mix_norm, w_in, b_forget, conv_w,
           w_o_attn, w_o_conv, w_out, ffn2_norm, ffn2_gate, ffn2_up, ffn2_down, final_norm):
    bsz, seq, d = x.shape
    assert d == D_MODEL and seq % TM_A == 0 and seq % TM_C == 0 and seq % TQ == 0
    row = lambda v: v.reshape(1, -1).astype(F32)
    w_a, w_gate = pl.pallas_call(
        _split_w_in_body,
        grid=(D_MODEL // W_IN_ROWS,),
        in_specs=[pl.BlockSpec((W_IN_ROWS, w_in.shape[1]), lambda i: (i, 0))],
        out_specs=[pl.BlockSpec((W_IN_ROWS, QKVF_COLS), lambda i: (i, 0)),
                   pl.BlockSpec((W_IN_ROWS, GATE_COLS), lambda i: (i, 0))],
        out_shape=[jax.ShapeDtypeStruct((D_MODEL, QKVF_COLS), BF16),
                   jax.ShapeDtypeStruct((D_MODEL, GATE_COLS), BF16)],
        compiler_params=pltpu.CompilerParams(dimension_semantics=("arbitrary",)),
        name="split_w_in",
    )(w_in)
    b_f = jnp.pad(row(b_forget), ((0, 0), (0, LANES - N_HEADS)))
    sel_q, sel_k = _bias_selectors()
    params = pltpu.CompilerParams(dimension_semantics=("arbitrary", "arbitrary"),
                                  vmem_limit_bytes=VMEM_LIMIT)

    tile_a = lambda w: pl.BlockSpec((None, TM_A, w), lambda b, i: (b, i, 0))
    x1, qk, vt, cum = pl.pallas_call(
        _ffn1_qkv_body,
        grid=(bsz, seq // TM_A),
        in_specs=[tile_a(D_MODEL), _resident((1, D_MODEL)),
                  _resident((D_MODEL, D_FF)), _resident((D_MODEL, D_FF)),
                  _resident((D_FF, D_MODEL)), _resident((1, D_MODEL)),
                  _resident((D_MODEL, QKVF_COLS)), _resident((1, LANES))],
        out_specs=[tile_a(D_MODEL), tile_a(2 * ATTN_W),
                   pl.BlockSpec((None, N_HEADS * VT_ROWS, TM_A), lambda b, i: (b, 0, i)),
                   tile_a(LANES)],
        out_shape=[jax.ShapeDtypeStruct((bsz, seq, D_MODEL), F32),
                   jax.ShapeDtypeStruct((bsz, seq, 2 * ATTN_W), BF16),
                   jax.ShapeDtypeStruct((bsz, N_HEADS * VT_ROWS, seq), BF16),
                   jax.ShapeDtypeStruct((bsz, seq, LANES), F32)],
        scratch_shapes=[pltpu.VMEM((TM_A, D_MODEL), BF16), pltpu.VMEM((TM_A, D_FF), BF16),
                        pltpu.VMEM((1, LANES), F32)],
        compiler_params=params,
        name="ffn1_qkv",
    )(x, row(ffn1_norm), ffn1_gate.astype(BF16), ffn1_up.astype(BF16),
      ffn1_down.astype(BF16), row(mix_norm), w_a, b_f)

    n_blk = seq // TQ
    y = pl.pallas_call(
        _fox_attn_body,
        grid=(bsz,),
        in_specs=[pl.BlockSpec((None, seq, 2 * ATTN_W), lambda b: (b, 0, 0)),
                  pl.BlockSpec((None, N_HEADS * VT_ROWS, seq), lambda b: (b, 0, 0)),
                  pl.BlockSpec((None, seq, LANES), lambda b: (b, 0, 0)),
                  _resident((N_SPLIT * LANES, ATTN_W)), _resident((N_SPLIT * LANES, ATTN_W))],
        out_specs=pl.BlockSpec((None, seq, ATTN_W), lambda b: (b, 0, 0)),
        out_shape=jax.ShapeDtypeStruct((bsz, seq, ATTN_W), BF16),
        scratch_shapes=[pltpu.VMEM((N_HEADS // 2, n_blk, 2 * TK, 2 * LANES), BF16),
                        pltpu.VMEM((n_blk, N_HEADS // 2, TQ, 2 * LANES), BF16),
                        pltpu.VMEM((ST_SLOTS, 2 * TK, TQ), F32),
                        pltpu.VMEM((2, N_HEADS, TQ), F32),
                        pltpu.VMEM((2, N_HEADS * VT_ROWS, TQ), F32),
                        pltpu.VMEM((2, ATTN_W, TQ), F32)],
        compiler_params=pltpu.CompilerParams(dimension_semantics=("arbitrary",),
                                             vmem_limit_bytes=VMEM_LIMIT),
        name="fox_attn",
    )(qk, vt, cum, sel_q, sel_k)

    tile_c = lambda w: pl.BlockSpec((None, TM_C, w), lambda b, i: (b, i, 0))
    return pl.pallas_call(
        _mix_ffn2_body,
        grid=(bsz, seq // TM_C),
        in_specs=[tile_c(D_MODEL), tile_c(ATTN_W), _resident((1, D_MODEL)),
                  _resident((D_MODEL, GATE_COLS)), _resident((3, CONV_W)),
                  _resident((ATTN_W, D_MODEL)), _resident((CONV_W, D_MODEL)),
                  _resident((D_MODEL, D_MODEL)), _resident((1, D_MODEL)),
                  _resident((D_MODEL, D_FF)), _resident((D_MODEL, D_FF)),
                  _resident((D_FF, D_MODEL)), _resident((1, D_MODEL))],
        out_specs=tile_c(D_MODEL),
        out_shape=jax.ShapeDtypeStruct((bsz, seq, D_MODEL), F32),
        scratch_shapes=[pltpu.VMEM((TM_C, D_MODEL), BF16), pltpu.VMEM((TM_C, D_FF), BF16),
                        pltpu.VMEM((2, CONV_W), F32)],
        compiler_params=params,
        name="mix_ffn2",
    )(x1, y, row(mix_norm), w_gate, conv_w.astype(F32), w_o_attn.astype(BF16),
      w_o_conv.astype(BF16), w_out.astype(BF16), row(ffn2_norm), ffn2_gate.astype(BF16),
      ffn2_up.astype(BF16), ffn2_down.astype(BF16), row(final_norm))
```

```python
import math

import numpy as np
import jax
import jax.numpy as jnp
from jax import lax
from jax.experimental import pallas as pl
from jax.experimental.pallas import tpu as pltpu

F32 = jnp.float32
BF16 = jnp.bfloat16

D_MODEL = 1024
N_HEADS = 8
HEAD_DIM = 64
ATTN_W = N_HEADS * HEAD_DIM
CONV_W = D_MODEL // 2
D_FF = 2816
RMS_EPS = 1e-6
FFN_RES = 0.5
LANES = 128
QKVF_COLS = 3 * ATTN_W + LANES
GATE_COLS = 3 * CONV_W + 2 * D_MODEL
FF_CHUNK = 256
TM_A = 1024
TM_C = 512
W_IN_ROWS = 256
SUB_A = 2
SUB_C = 2
TQ = 256
TK = TQ
SCORES_AHEAD = 5
LOG2E = 1.4426950408889634
NEG_BIG = -1e30
VT_ROWS = HEAD_DIM + 16
N_SPLIT = 3
BIAS_SLOT = 8
ONES_LANE = N_HEADS
VMEM_LIMIT = 56 * 1024 * 1024


def _rms(x, g):
    inv = lax.rsqrt(jnp.mean(x * x, axis=-1, keepdims=True) + RMS_EPS)
    return (x * inv) * g


def _sub_rows(tm, n_sub):
    th = tm // n_sub
    return [slice(i * th, (i + 1) * th) for i in range(n_sub)]


def _swiglu_act(h_scr, act_scr, wg_ref, wu_ref, subs, fill_h=None):
    for c in range(D_FF // FF_CHUNK):
        sl = slice(c * FF_CHUNK, (c + 1) * FF_CHUNK)
        for rows in subs:
            if c == 0 and fill_h is not None:
                fill_h(rows)
            g = jnp.dot(h_scr[rows, :], wg_ref[:, sl], preferred_element_type=F32)
            u = jnp.dot(h_scr[rows, :], wu_ref[:, sl], preferred_element_type=F32)
            act_scr[rows, sl] = (g * jax.nn.sigmoid(g) * u).astype(BF16)


def _cumsum_rows(x):
    n = x.shape[0]
    row = lax.broadcasted_iota(jnp.int32, x.shape, 0)
    d = 1
    while d < n:
        x = x + jnp.where(row >= d, pltpu.roll(x, d, axis=0), 0.0)
        d *= 2
    return x


def _ffn1_qkv_body(x_ref, g1_ref, wg_ref, wu_ref, wd_ref, g2_ref, wa_ref, bf_ref,
                   x1_ref, qk_ref, vt_ref, cc_ref,
                   h_scr, act_scr, carry_scr):
    subs = _sub_rows(TM_A, SUB_A)

    def fill_h(rows):
        h_scr[rows, :] = _rms(x_ref[rows, :], g1_ref[...]).astype(BF16)
    _swiglu_act(h_scr, act_scr, wg_ref, wu_ref, subs, fill_h)
    for rows in subs:
        x1 = x_ref[rows, :] + FFN_RES * jnp.dot(act_scr[rows, :], wd_ref[...],
                                                preferred_element_type=F32)
        x1_ref[rows, :] = x1
        h_scr[rows, :] = _rms(x1, g2_ref[...]).astype(BF16)

    @pl.when(pl.program_id(1) == 0)
    def _():
        carry_scr[...] = jnp.zeros_like(carry_scr)

    carry = carry_scr[...]
    for rows in subs:
        pr = jnp.dot(h_scr[rows, :], wa_ref[...], preferred_element_type=F32)
        qk_ref[rows, :ATTN_W] = (pr[:, :ATTN_W] * (LOG2E / math.sqrt(HEAD_DIM))).astype(BF16)
        qk_ref[rows, ATTN_W:] = pr[:, ATTN_W:2 * ATTN_W].astype(BF16)
        v_t = pr[:, 2 * ATTN_W:3 * ATTN_W].T.astype(BF16)
        for h in range(N_HEADS):
            vt_ref[h * VT_ROWS:h * VT_ROWS + HEAD_DIM, rows] = v_t[h * HEAD_DIM:(h + 1) * HEAD_DIM, :]
            vt_ref[h * VT_ROWS + HEAD_DIM:(h + 1) * VT_ROWS, rows] = jnp.ones(
                (VT_ROWS - HEAD_DIM, v_t.shape[1]), BF16)
        z = pr[:, 3 * ATTN_W:] + bf_ref[...]
        log_f = jnp.minimum(z, 0.0) - jnp.log1p(jnp.exp(-jnp.abs(z)))
        cum = _cumsum_rows(log_f) + carry
        carry = cum[cum.shape[0] - 1:, :]
        cc_ref[rows, :] = cum * LOG2E
    carry_scr[...] = carry


def _split_cum(c):
    pieces = []
    r = c
    for _ in range(N_SPLIT):
        p = r.astype(BF16)
        pieces.append(p)
        r = r - p.astype(F32)
    lane = lax.broadcasted_iota(jnp.int32, c.shape, 1)
    pieces[0] = jnp.where(lane == ONES_LANE, jnp.ones_like(pieces[0]), pieces[0])
    return jnp.concatenate(pieces, axis=1)


def _bias_selectors():
    sq = np.zeros((N_SPLIT * LANES, ATTN_W), np.float32)
    sk = np.zeros((N_SPLIT * LANES, ATTN_W), np.float32)
    for h in range(N_HEADS):
        base = (h // 2) * LANES + (h % 2) * BIAS_SLOT
        for s in range(N_SPLIT):
            sq[s * LANES + h, base + s] = 1.0
            sq[ONES_LANE, base + N_SPLIT + s] = 1.0
            sk[ONES_LANE, base + s] = 1.0
            sk[s * LANES + h, base + N_SPLIT + s] = -1.0
    return jnp.asarray(sq, BF16), jnp.asarray(sk, BF16)


def _fox_attn_body(qk_ref, vt_ref, cc_ref, sq_ref, sk_ref, y_ref,
                   kf_scr, qc_scr, m_scr, ot_scr, yt_scr):
    n_blk = qk_ref.shape[0] // TQ
    n_pairs = N_HEADS // 2
    lane = lax.broadcasted_iota(jnp.int32, (TK, LANES), 1)

    for r in range(n_blk):
        rows_r = slice(r * TK, (r + 1) * TK)
        pieces = _split_cum(cc_ref[rows_r, :])
        ka = jnp.dot(pieces, sk_ref[...], preferred_element_type=F32).astype(BF16)
        qa = jnp.dot(pieces, sq_ref[...], preferred_element_type=F32).astype(BF16)
        for pair in range(n_pairs):
            cols = slice(pair * LANES, (pair + 1) * LANES)
            kp = qk_ref[rows_r, ATTN_W + pair * LANES:ATTN_W + (pair + 1) * LANES]
            kap = ka[:, cols]
            for hh in range(2):
                own = (lane >= hh * HEAD_DIM) & (lane < (hh + 1) * HEAD_DIM)
                own_b = (lane >= hh * BIAS_SLOT) & (lane < (hh + 1) * BIAS_SLOT)
                rows = slice(hh * TK, (hh + 1) * TK)
                kf_scr[pair, r, rows, :LANES] = jnp.where(own, kp, jnp.zeros_like(kp))
                kf_scr[pair, r, rows, LANES:] = jnp.where(own_b, kap, jnp.zeros_like(kap))
            qc_scr[r, pair, :, :LANES] = qk_ref[rows_r, cols]
            qc_scr[r, pair, :, LANES:] = qa[:, cols]

    k_idx = lax.broadcasted_iota(jnp.int32, (TK, TQ), 0)
    q_idx = lax.broadcasted_iota(jnp.int32, (TK, TQ), 1)

    stream = [(qi, j, pair) for qi in range(n_blk) for j in range(qi + 1)
              for pair in range(n_pairs)]

    pending = {}

    def scores(n):
        qi, j, pair = stream[n]
        pending[n] = lax.dot_general(kf_scr[pair, j], qc_scr[qi, pair],
                                     (((1,), (1,)), ((), ())), preferred_element_type=F32)

    def head(st_pair, par, h, j, on_diagonal):
        st = st_pair[(h % 2) * TK:(h % 2 + 1) * TK, :]
        if on_diagonal:
            st = jnp.where(q_idx >= k_idx, st, NEG_BIG)
        m_old = m_scr[par, h:h + 1, :]
        m_new = jnp.maximum(m_old, jnp.max(st, axis=0, keepdims=True))
        alpha = jnp.exp2(m_old - m_new)
        p = jnp.exp2(st - m_new).astype(BF16)
        m_scr[par, h:h + 1, :] = m_new
        rows = slice(h * VT_ROWS, (h + 1) * VT_ROWS)
        ot_scr[par, rows, :] = alpha * ot_scr[par, rows, :] + jnp.dot(
            vt_ref[rows, j * TK:(j + 1) * TK], p, preferred_element_type=F32)

    for n in range(SCORES_AHEAD):
        scores(n)
    for n, (qi, j, pair) in enumerate(stream):
        par = qi % 2
        if n + SCORES_AHEAD < len(stream):
            scores(n + SCORES_AHEAD)
        if j == 0 and pair == 0:
            m_scr[par] = jnp.full(m_scr.shape[1:], NEG_BIG, F32)
            ot_scr[par] = jnp.zeros(ot_scr.shape[1:], F32)
        st_pair = pending.pop(n)
        head(st_pair, par, 2 * pair, j, j == qi)
        head(st_pair, par, 2 * pair + 1, j, j == qi)
        if j == qi and pair == n_pairs - 1:
            for h in range(N_HEADS):
                o = ot_scr[par, h * VT_ROWS:h * VT_ROWS + HEAD_DIM, :]
                l = ot_scr[par, h * VT_ROWS + HEAD_DIM:h * VT_ROWS + HEAD_DIM + 1, :]
                yt_scr[par, h * HEAD_DIM:(h + 1) * HEAD_DIM, :] = o / l
            y_ref[qi * TQ:(qi + 1) * TQ, :] = yt_scr[par].T.astype(BF16)


def _mix_ffn2_body(x1_ref, y_ref, gm_ref, wgate_ref, cw_ref, woa_ref, woc_ref, wout_ref,
                   g3_ref, wg_ref, wu_ref, wd_ref, gf_ref,
                   o_ref, h_scr, act_scr, tail_scr):
    subs = _sub_rows(TM_C, SUB_C)
    th = TM_C // SUB_C

    def gate(rows, lo, hi):
        return jnp.dot(h_scr[rows, :], wgate_ref[:, lo:hi], preferred_element_type=F32)

    @pl.when(pl.program_id(1) == 0)
    def _():
        tail_scr[...] = jnp.zeros_like(tail_scr)

    row = lax.broadcasted_iota(jnp.int32, (th, CONV_W), 0)
    tail = tail_scr[...]
    mixed = []
    for rows in subs:
        h_scr[rows, :] = _rms(x1_ref[rows, :], gm_ref[...]).astype(BF16)
        c_b = gate(rows, 0, CONV_W)
        u = gate(rows, CONV_W, 2 * CONV_W) * gate(rows, 2 * CONV_W, 3 * CONV_W)
        prev2, prev1 = tail[0:1, :], tail[1:2, :]
        u1 = jnp.where(row == 0, prev1, pltpu.roll(u, 1, axis=0))
        u2 = jnp.where(row == 0, prev2, jnp.where(row == 1, prev1, pltpu.roll(u, 2, axis=0)))
        tail = u[th - 2:th, :]
        conv = cw_ref[0:1, :] * u2 + cw_ref[1:2, :] * u1 + cw_ref[2:3, :] * u
        mixed.append((c_b * conv).astype(BF16))
    tail_scr[...] = tail

    o0 = 3 * CONV_W
    for rows, z in zip(subs, mixed):
        y_conv = jnp.dot(z, woc_ref[...], preferred_element_type=F32)
        y_attn = jnp.dot(y_ref[rows, :], woa_ref[...], preferred_element_type=F32)
        merged = (jax.nn.sigmoid(gate(rows, o0, o0 + D_MODEL)) * y_attn
                  + jax.nn.sigmoid(gate(rows, o0 + D_MODEL, o0 + 2 * D_MODEL)) * y_conv)
        act_scr[rows, :D_MODEL] = merged.astype(BF16)
    for rows in subs:
        x2 = x1_ref[rows, :] + jnp.dot(act_scr[rows, :D_MODEL], wout_ref[...],
                                       preferred_element_type=F32)
        o_ref[rows, :] = x2
        h_scr[rows, :] = _rms(x2, g3_ref[...]).astype(BF16)

    _swiglu_act(h_scr, act_scr, wg_ref, wu_ref, subs)
    for rows in subs:
        x3 = o_ref[rows, :] + FFN_RES * jnp.dot(act_scr[rows, :], wd_ref[...],
                                                preferred_element_type=F32)
        o_ref[rows, :] = _rms(x3, gf_ref[...])


def _split_w_in_body(w_ref, wa_ref, wg_ref):
    n_qkv = 3 * ATTN_W
    wa_ref[:, :n_qkv] = w_ref[:, :n_qkv].astype(BF16)
    tail = w_ref[:, n_qkv:n_qkv + LANES]
    lane = lax.broadcasted_iota(jnp.int32, tail.shape, 1)
    wa_ref[:, n_qkv:] = jnp.where(lane < N_HEADS, tail, 0.0).astype(BF16)
    wg_ref[...] = w_ref[:, n_qkv + N_HEADS:].astype(BF16)


def _resident(shape):
    return pl.BlockSpec(shape, lambda *_: (0,) * len(shape), pipeline_mode=pl.Buffered(1))


def kernel(x, ffn1_norm, ffn1_gate, ffn1_up, ffn1_down, mix_norm, w_in, b_forget, conv_w,
           w_o_attn, w_o_conv, w_out, ffn2_norm, ffn2_gate, ffn2_up, ffn2_down, final_norm):
    bsz, seq, d = x.shape
    assert d == D_MODEL and seq % TM_A == 0 and seq % TM_C == 0 and seq % TQ == 0
    row = lambda v: v.reshape(1, -1).astype(F32)
    w_a, w_gate = pl.pallas_call(
        _split_w_in_body,
        grid=(D_MODEL // W_IN_ROWS,),
        in_specs=[pl.BlockSpec((W_IN_ROWS, w_in.shape[1]), lambda i: (i, 0))],
        out_specs=[pl.BlockSpec((W_IN_ROWS, QKVF_COLS), lambda i: (i, 0)),
                   pl.BlockSpec((W_IN_ROWS, GATE_COLS), lambda i: (i, 0))],
        out_shape=[jax.ShapeDtypeStruct((D_MODEL, QKVF_COLS), BF16),
                   jax.ShapeDtypeStruct((D_MODEL, GATE_COLS), BF16)],
        compiler_params=pltpu.CompilerParams(dimension_semantics=("arbitrary",)),
        name="split_w_in",
    )(w_in)
    b_f = jnp.pad(row(b_forget), ((0, 0), (0, LANES - N_HEADS)))
    sel_q, sel_k = _bias_selectors()
    params = pltpu.CompilerParams(dimension_semantics=("arbitrary", "arbitrary"),
                                  vmem_limit_bytes=VMEM_LIMIT)

    tile_a = lambda w: pl.BlockSpec((None, TM_A, w), lambda b, i: (b, i, 0))
    x1, qk, vt, cum = pl.pallas_call(
        _ffn1_qkv_body,
        grid=(bsz, seq // TM_A),
        in_specs=[tile_a(D_MODEL), _resident((1, D_MODEL)),
                  _resident((D_MODEL, D_FF)), _resident((D_MODEL, D_FF)),
                  _resident((D_FF, D_MODEL)), _resident((1, D_MODEL)),
                  _resident((D_MODEL, QKVF_COLS)), _resident((1, LANES))],
        out_specs=[tile_a(D_MODEL), tile_a(2 * ATTN_W),
                   pl.BlockSpec((None, N_HEADS * VT_ROWS, TM_A), lambda b, i: (b, 0, i)),
                   tile_a(LANES)],
        out_shape=[jax.ShapeDtypeStruct((bsz, seq, D_MODEL), F32),
                   jax.ShapeDtypeStruct((bsz, seq, 2 * ATTN_W), BF16),
                   jax.ShapeDtypeStruct((bsz, N_HEADS * VT_ROWS, seq), BF16),
                   jax.ShapeDtypeStruct((bsz, seq, LANES), F32)],
        scratch_shapes=[pltpu.VMEM((TM_A, D_MODEL), BF16), pltpu.VMEM((TM_A, D_FF), BF16),
                        pltpu.VMEM((1, LANES), F32)],
        compiler_params=params,
        name="ffn1_qkv",
    )(x, row(ffn1_norm), ffn1_gate.astype(BF16), ffn1_up.astype(BF16),
      ffn1_down.astype(BF16), row(mix_norm), w_a, b_f)

    n_blk = seq // TQ
    y = pl.pallas_call(
        _fox_attn_body,
        grid=(bsz,),
        in_specs=[pl.BlockSpec((None, seq, 2 * ATTN_W), lambda b: (b, 0, 0)),
                  pl.BlockSpec((None, N_HEADS * VT_ROWS, seq), lambda b: (b, 0, 0)),
                  pl.BlockSpec((None, seq, LANES), lambda b: (b, 0, 0)),
                  _resident((N_SPLIT * LANES, ATTN_W)), _resident((N_SPLIT * LANES, ATTN_W))],
        out_specs=pl.BlockSpec((None, seq, ATTN_W), lambda b: (b, 0, 0)),
        out_shape=jax.ShapeDtypeStruct((bsz, seq, ATTN_W), BF16),
        scratch_shapes=[pltpu.VMEM((N_HEADS // 2, n_blk, 2 * TK, 2 * LANES), BF16),
                        pltpu.VMEM((n_blk, N_HEADS // 2, TQ, 2 * LANES), BF16),
                        pltpu.VMEM((2, N_HEADS, TQ), F32),
                        pltpu.VMEM((2, N_HEADS * VT_ROWS, TQ), F32),
                        pltpu.VMEM((2, ATTN_W, TQ), F32)],
        compiler_params=pltpu.CompilerParams(dimension_semantics=("arbitrary",),
                                             vmem_limit_bytes=VMEM_LIMIT),
        name="fox_attn",
    )(qk, vt, cum, sel_q, sel_k)

    tile_c = lambda w: pl.BlockSpec((None, TM_C, w), lambda b, i: (b, i, 0))
    return pl.pallas_call(
        _mix_ffn2_body,
        grid=(bsz, seq // TM_C),
        in_specs=[tile_c(D_MODEL), tile_c(ATTN_W), _resident((1, D_MODEL)),
                  _resident((D_MODEL, GATE_COLS)), _resident((3, CONV_W)),
                  _resident((ATTN_W, D_MODEL)), _resident((CONV_W, D_MODEL)),
                  _resident((D_MODEL, D_MODEL)), _resident((1, D_MODEL)),
                  _resident((D_MODEL, D_FF)), _resident((D_MODEL, D_FF)),
                  _resident((D_FF, D_MODEL)), _resident((1, D_MODEL))],
        out_specs=tile_c(D_MODEL),
        out_shape=jax.ShapeDtypeStruct((bsz, seq, D_MODEL), F32),
        scratch_shapes=[pltpu.VMEM((TM_C, D_MODEL), BF16), pltpu.VMEM((TM_C, D_FF), BF16),
                        pltpu.VMEM((2, CONV_W), F32)],
        compiler_params=params,
        name="mix_ffn2",
    )(x1, y, row(mix_norm), w_gate, conv_w.astype(F32), w_o_attn.astype(BF16),
      w_o_conv.astype(BF16), w_out.astype(BF16), row(ffn2_norm), ffn2_gate.astype(BF16),
      ffn2_up.astype(BF16), ffn2_down.astype(BF16), row(final_norm))
```

```python
import math

import numpy as np
import jax
import jax.numpy as jnp
from jax import lax
from jax.experimental import pallas as pl
from jax.experimental.pallas import tpu as pltpu

F32 = jnp.float32
BF16 = jnp.bfloat16

D_MODEL = 1024
N_HEADS = 8
HEAD_DIM = 64
ATTN_W = N_HEADS * HEAD_DIM
CONV_W = D_MODEL // 2
D_FF = 2816
RMS_EPS = 1e-6
FFN_RES = 0.5
LANES = 128
QKVF_COLS = 3 * ATTN_W + LANES
GATE_COLS = 3 * CONV_W + 2 * D_MODEL
FF_CHUNK = 256
TM_A = 1024
TM_C = 1024
W_IN_ROWS = 256
SUB_A = 2
SUB_C = 4
TQ = 256
TK = TQ
SCORES_AHEAD = 2
LOG2E = 1.4426950408889634
NEG_BIG = -1e30
VT_ROWS = HEAD_DIM + 16
N_SPLIT = 3
BIAS_SLOT = 8
ONES_LANE = N_HEADS
VMEM_LIMIT = 56 * 1024 * 1024


def _rms(x, g):
    inv = lax.rsqrt(jnp.mean(x * x, axis=-1, keepdims=True) + RMS_EPS)
    return (x * inv) * g


def _sub_rows(tm, n_sub):
    th = tm // n_sub
    return [slice(i * th, (i + 1) * th) for i in range(n_sub)]


def _swiglu_act(h_scr, act_scr, wg_ref, wu_ref, subs, fill_h=None):
    for c in range(D_FF // FF_CHUNK):
        sl = slice(c * FF_CHUNK, (c + 1) * FF_CHUNK)
        for rows in subs:
            if c == 0 and fill_h is not None:
                fill_h(rows)
            g = jnp.dot(h_scr[rows, :], wg_ref[:, sl], preferred_element_type=F32)
            u = jnp.dot(h_scr[rows, :], wu_ref[:, sl], preferred_element_type=F32)
            act_scr[rows, sl] = (g * jax.nn.sigmoid(g) * u).astype(BF16)


def _cumsum_rows(x):
    n = x.shape[0]
    row = lax.broadcasted_iota(jnp.int32, x.shape, 0)
    d = 1
    while d < n:
        x = x + jnp.where(row >= d, pltpu.roll(x, d, axis=0), 0.0)
        d *= 2
    return x


def _ffn1_qkv_body(x_ref, g1_ref, wg_ref, wu_ref, wd_ref, g2_ref, wa_ref, bf_ref,
                   x1_ref, qk_ref, vt_ref, cc_ref,
                   h_scr, act_scr, carry_scr):
    subs = _sub_rows(TM_A, SUB_A)

    def fill_h(rows):
        h_scr[rows, :] = _rms(x_ref[rows, :], g1_ref[...]).astype(BF16)
    _swiglu_act(h_scr, act_scr, wg_ref, wu_ref, subs, fill_h)
    for rows in subs:
        x1 = x_ref[rows, :] + FFN_RES * jnp.dot(act_scr[rows, :], wd_ref[...],
                                                preferred_element_type=F32)
        x1_ref[rows, :] = x1
        h_scr[rows, :] = _rms(x1, g2_ref[...]).astype(BF16)

    @pl.when(pl.program_id(1) == 0)
    def _():
        carry_scr[...] = jnp.zeros_like(carry_scr)

    carry = carry_scr[...]
    for rows in subs:
        pr = jnp.dot(h_scr[rows, :], wa_ref[...], preferred_element_type=F32)
        qk_ref[rows, :ATTN_W] = (pr[:, :ATTN_W] * (LOG2E / math.sqrt(HEAD_DIM))).astype(BF16)
        qk_ref[rows, ATTN_W:] = pr[:, ATTN_W:2 * ATTN_W].astype(BF16)
        v_t = pr[:, 2 * ATTN_W:3 * ATTN_W].T.astype(BF16)
        for h in range(N_HEADS):
            vt_ref[h * VT_ROWS:h * VT_ROWS + HEAD_DIM, rows] = v_t[h * HEAD_DIM:(h + 1) * HEAD_DIM, :]
            vt_ref[h * VT_ROWS + HEAD_DIM:(h + 1) * VT_ROWS, rows] = jnp.ones(
                (VT_ROWS - HEAD_DIM, v_t.shape[1]), BF16)
        z = pr[:, 3 * ATTN_W:] + bf_ref[...]
        log_f = jnp.minimum(z, 0.0) - jnp.log1p(jnp.exp(-jnp.abs(z)))
        cum = _cumsum_rows(log_f) + carry
        carry = cum[cum.shape[0] - 1:, :]
        cc_ref[rows, :] = cum * LOG2E
    carry_scr[...] = carry


def _split_cum(c):
    pieces = []
    r = c
    for _ in range(N_SPLIT):
        p = r.astype(BF16)
        pieces.append(p)
        r = r - p.astype(F32)
    lane = lax.broadcasted_iota(jnp.int32, c.shape, 1)
    pieces[0] = jnp.where(lane == ONES_LANE, jnp.ones_like(pieces[0]), pieces[0])
    return jnp.concatenate(pieces, axis=1)


def _bias_selectors():
    sq = np.zeros((N_SPLIT * LANES, ATTN_W), np.float32)
    sk = np.zeros((N_SPLIT * LANES, ATTN_W), np.float32)
    for h in range(N_HEADS):
        base = (h // 2) * LANES + (h % 2) * BIAS_SLOT
        for s in range(N_SPLIT):
            sq[s * LANES + h, base + s] = 1.0
            sq[ONES_LANE, base + N_SPLIT + s] = 1.0
            sk[ONES_LANE, base + s] = 1.0
            sk[s * LANES + h, base + N_SPLIT + s] = -1.0
    return jnp.asarray(sq, BF16), jnp.asarray(sk, BF16)


def _fox_attn_body(qk_ref, vt_ref, cc_ref, sq_ref, sk_ref, y_ref,
                   kf_scr, qc_scr, m_scr, ot_scr, yt_scr):
    n_blk = qk_ref.shape[0] // TQ
    n_pairs = N_HEADS // 2
    lane = lax.broadcasted_iota(jnp.int32, (TK, LANES), 1)

    for r in range(n_blk):
        rows_r = slice(r * TK, (r + 1) * TK)
        pieces = _split_cum(cc_ref[rows_r, :])
        ka = jnp.dot(pieces, sk_ref[...], preferred_element_type=F32).astype(BF16)
        qa = jnp.dot(pieces, sq_ref[...], preferred_element_type=F32).astype(BF16)
        for pair in range(n_pairs):
            cols = slice(pair * LANES, (pair + 1) * LANES)
            kp = qk_ref[rows_r, ATTN_W + pair * LANES:ATTN_W + (pair + 1) * LANES]
            kap = ka[:, cols]
            for hh in range(2):
                own = (lane >= hh * HEAD_DIM) & (lane < (hh + 1) * HEAD_DIM)
                own_b = (lane >= hh * BIAS_SLOT) & (lane < (hh + 1) * BIAS_SLOT)
                rows = slice(hh * TK, (hh + 1) * TK)
                kf_scr[pair, r, rows, :LANES] = jnp.where(own, kp, jnp.zeros_like(kp))
                kf_scr[pair, r, rows, LANES:] = jnp.where(own_b, kap, jnp.zeros_like(kap))
            qc_scr[r, pair, :, :LANES] = qk_ref[rows_r, cols]
            qc_scr[r, pair, :, LANES:] = qa[:, cols]

    k_idx = lax.broadcasted_iota(jnp.int32, (TK, TQ), 0)
    q_idx = lax.broadcasted_iota(jnp.int32, (TK, TQ), 1)

    stream = [(qi, j, pair) for qi in range(n_blk) for j in range(qi + 1)
              for pair in range(n_pairs)]

    pending = {}

    def scores(n):
        qi, j, pair = stream[n]
        pending[n] = lax.dot_general(kf_scr[pair, j], qc_scr[qi, pair],
                                     (((1,), (1,)), ((), ())), preferred_element_type=F32)

    def head(st_pair, par, h, j, on_diagonal):
        st = st_pair[(h % 2) * TK:(h % 2 + 1) * TK, :]
        if on_diagonal:
            st = jnp.where(q_idx >= k_idx, st, NEG_BIG)
        m_old = m_scr[par, h:h + 1, :]
        m_new = jnp.maximum(m_old, jnp.max(st, axis=0, keepdims=True))
        alpha = jnp.exp2(m_old - m_new)
        p = jnp.exp2(st - m_new).astype(BF16)
        m_scr[par, h:h + 1, :] = m_new
        rows = slice(h * VT_ROWS, (h + 1) * VT_ROWS)
        ot_scr[par, rows, :] = alpha * ot_scr[par, rows, :] + jnp.dot(
            vt_ref[rows, j * TK:(j + 1) * TK], p, preferred_element_type=F32)

    for n in range(SCORES_AHEAD):
        scores(n)
    for n, (qi, j, pair) in enumerate(stream):
        par = qi % 2
        if n + SCORES_AHEAD < len(stream):
            scores(n + SCORES_AHEAD)
        if j == 0 and pair == 0:
            m_scr[par] = jnp.full(m_scr.shape[1:], NEG_BIG, F32)
            ot_scr[par] = jnp.zeros(ot_scr.shape[1:], F32)
        st_pair = pending.pop(n)
        head(st_pair, par, 2 * pair, j, j == qi)
        head(st_pair, par, 2 * pair + 1, j, j == qi)
        if j == qi and pair == n_pairs - 1:
            for h in range(N_HEADS):
                o = ot_scr[par, h * VT_ROWS:h * VT_ROWS + HEAD_DIM, :]
                l = ot_scr[par, h * VT_ROWS + HEAD_DIM:h * VT_ROWS + HEAD_DIM + 1, :]
                yt_scr[par, h * HEAD_DIM:(h + 1) * HEAD_DIM, :] = o / l
            y_ref[qi * TQ:(qi + 1) * TQ, :] = yt_scr[par].T.astype(BF16)


def _mix_ffn2_body(x1_ref, y_ref, gm_ref, wgate_ref, cw_ref, woa_ref, woc_ref, wout_ref,
                   g3_ref, wg_ref, wu_ref, wd_ref, gf_ref,
                   o_ref, h_scr, act_scr, tail_scr):
    subs = _sub_rows(TM_C, SUB_C)
    th = TM_C // SUB_C

    def gate(rows, lo, hi):
        return jnp.dot(h_scr[rows, :], wgate_ref[:, lo:hi], preferred_element_type=F32)

    @pl.when(pl.program_id(1) == 0)
    def _():
        tail_scr[...] = jnp.zeros_like(tail_scr)

    row = lax.broadcasted_iota(jnp.int32, (th, CONV_W), 0)
    tail = tail_scr[...]
    mixed = []
    for rows in subs:
        h_scr[rows, :] = _rms(x1_ref[rows, :], gm_ref[...]).astype(BF16)
        c_b = gate(rows, 0, CONV_W)
        u = gate(rows, CONV_W, 2 * CONV_W) * gate(rows, 2 * CONV_W, 3 * CONV_W)
        prev2, prev1 = tail[0:1, :], tail[1:2, :]
        u1 = jnp.where(row == 0, prev1, pltpu.roll(u, 1, axis=0))
        u2 = jnp.where(row == 0, prev2, jnp.where(row == 1, prev1, pltpu.roll(u, 2, axis=0)))
        tail = u[th - 2:th, :]
        conv = cw_ref[0:1, :] * u2 + cw_ref[1:2, :] * u1 + cw_ref[2:3, :] * u
        mixed.append((c_b * conv).astype(BF16))
    tail_scr[...] = tail

    o0 = 3 * CONV_W
    for rows, z in zip(subs, mixed):
        y_conv = jnp.dot(z, woc_ref[...], preferred_element_type=F32)
        y_attn = jnp.dot(y_ref[rows, :], woa_ref[...], preferred_element_type=F32)
        merged = (jax.nn.sigmoid(gate(rows, o0, o0 + D_MODEL)) * y_attn
                  + jax.nn.sigmoid(gate(rows, o0 + D_MODEL, o0 + 2 * D_MODEL)) * y_conv)
        act_scr[rows, :D_MODEL] = merged.astype(BF16)
    for rows in subs:
        x2 = x1_ref[rows, :] + jnp.dot(act_scr[rows, :D_MODEL], wout_ref[...],
                                       preferred_element_type=F32)
        o_ref[rows, :] = x2
        h_scr[rows, :] = _rms(x2, g3_ref[...]).astype(BF16)

    _swiglu_act(h_scr, act_scr, wg_ref, wu_ref, subs)
    for rows in subs:
        x3 = o_ref[rows, :] + FFN_RES * jnp.dot(act_scr[rows, :], wd_ref[...],
                                                preferred_element_type=F32)
        o_ref[rows, :] = _rms(x3, gf_ref[...])


def _split_w_in_body(w_ref, wa_ref, wg_ref):
    n_qkv = 3 * ATTN_W
    wa_ref[:, :n_qkv] = w_ref[:, :n_qkv].astype(BF16)
    tail = w_ref[:, n_qkv:n_qkv + LANES]
    lane = lax.broadcasted_iota(jnp.int32, tail.shape, 1)
    wa_ref[:, n_qkv:] = jnp.where(lane < N_HEADS, tail, 0.0).astype(BF16)
    wg_ref[...] = w_ref[:, n_qkv + N_HEADS:].astype(BF16)


def _resident(shape):
    return pl.BlockSpec(shape, lambda *_: (0,) * len(shape), pipeline_mode=pl.Buffered(1))


def kernel(x, ffn1_norm, ffn1_gate, ffn1_up, ffn1_down, mix_norm, w_in, b_forget, conv_w,
           w_o_attn, w_o_conv, w_out, ffn2_norm, ffn2_gate, ffn2_up, ffn2_down, final_norm):
    bsz, seq, d = x.shape
    assert d == D_MODEL and seq % TM_A == 0 and seq % TM_C == 0 and seq % TQ == 0
    row = lambda v: v.reshape(1, -1).astype(F32)
    w_a, w_gate = pl.pallas_call(
        _split_w_in_body,
        grid=(D_MODEL // W_IN_ROWS,),
        in_specs=[pl.BlockSpec((W_IN_ROWS, w_in.shape[1]), lambda i: (i, 0))],
        out_specs=[pl.BlockSpec((W_IN_ROWS, QKVF_COLS), lambda i: (i, 0)),
                   pl.BlockSpec((W_IN_ROWS, GATE_COLS), lambda i: (i, 0))],
        out_shape=[jax.ShapeDtypeStruct((D_MODEL, QKVF_COLS), BF16),
                   jax.ShapeDtypeStruct((D_MODEL, GATE_COLS), BF16)],
        compiler_params=pltpu.CompilerParams(dimension_semantics=("arbitrary",)),
        name="split_w_in",
    )(w_in)
    b_f = jnp.pad(row(b_forget), ((0, 0), (0, LANES - N_HEADS)))
    sel_q, sel_k = _bias_selectors()
    params = pltpu.CompilerParams(dimension_semantics=("arbitrary", "arbitrary"),
                                  vmem_limit_bytes=VMEM_LIMIT)

    tile_a = lambda w: pl.BlockSpec((None, TM_A, w), lambda b, i: (b, i, 0))
    x1, qk, vt, cum = pl.pallas_call(
        _ffn1_qkv_body,
        grid=(bsz, seq // TM_A),
        in_specs=[tile_a(D_MODEL), _resident((1, D_MODEL)),
                  _resident((D_MODEL, D_FF)), _resident((D_MODEL, D_FF)),
                  _resident((D_FF, D_MODEL)), _resident((1, D_MODEL)),
                  _resident((D_MODEL, QKVF_COLS)), _resident((1, LANES))],
        out_specs=[tile_a(D_MODEL), tile_a(2 * ATTN_W),
                   pl.BlockSpec((None, N_HEADS * VT_ROWS, TM_A), lambda b, i: (b, 0, i)),
                   tile_a(LANES)],
        out_shape=[jax.ShapeDtypeStruct((bsz, seq, D_MODEL), F32),
                   jax.ShapeDtypeStruct((bsz, seq, 2 * ATTN_W), BF16),
                   jax.ShapeDtypeStruct((bsz, N_HEADS * VT_ROWS, seq), BF16),
                   jax.ShapeDtypeStruct((bsz, seq, LANES), F32)],
        scratch_shapes=[pltpu.VMEM((TM_A, D_MODEL), BF16), pltpu.VMEM((TM_A, D_FF), BF16),
                        pltpu.VMEM((1, LANES), F32)],
        compiler_params=params,
        name="ffn1_qkv",
    )(x, row(ffn1_norm), ffn1_gate.astype(BF16), ffn1_up.astype(BF16),
      ffn1_down.astype(BF16), row(mix_norm), w_a, b_f)

    n_blk = seq // TQ
    y = pl.pallas_call(
        _fox_attn_body,
        grid=(bsz,),
        in_specs=[pl.BlockSpec((None, seq, 2 * ATTN_W), lambda b: (b, 0, 0)),
                  pl.BlockSpec((None, N_HEADS * VT_ROWS, seq), lambda b: (b, 0, 0)),
                  pl.BlockSpec((None, seq, LANES), lambda b: (b, 0, 0)),
                  _resident((N_SPLIT * LANES, ATTN_W)), _resident((N_SPLIT * LANES, ATTN_W))],
        out_specs=pl.BlockSpec((None, seq, ATTN_W), lambda b: (b, 0, 0)),
        out_shape=jax.ShapeDtypeStruct((bsz, seq, ATTN_W), BF16),
        scratch_shapes=[pltpu.VMEM((N_HEADS // 2, n_blk, 2 * TK, 2 * LANES), BF16),
                        pltpu.VMEM((n_blk, N_HEADS // 2, TQ, 2 * LANES), BF16),
                        pltpu.VMEM((2, N_HEADS, TQ), F32),
                        pltpu.VMEM((2, N_HEADS * VT_ROWS, TQ), F32),
                        pltpu.VMEM((2, ATTN_W, TQ), F32)],
        compiler_params=pltpu.CompilerParams(dimension_semantics=("arbitrary",),
                                             vmem_limit_bytes=VMEM_LIMIT),
        name="fox_attn",
    )(qk, vt, cum, sel_q, sel_k)

    tile_c = lambda w: pl.BlockSpec((None, TM_C, w), lambda b, i: (b, i, 0))
    return pl.pallas_call(
        _mix_ffn2_body,
        grid=(bsz, seq // TM_C),
        in_specs=[tile_c(D_MODEL), tile_c(ATTN_W), _resident((1, D_MODEL)),
                  _resident((D_MODEL, GATE_COLS)), _resident((3, CONV_W)),
                  _resident((ATTN_W, D_MODEL)), _resident((CONV_W, D_MODEL)),
                  _resident((D_MODEL, D_MODEL)), _resident((1, D_MODEL)),
                  _resident((D_MODEL, D_FF)), _resident((D_MODEL, D_FF)),
                  _resident((D_FF, D_MODEL)), _resident((1, D_MODEL))],
        out_specs=tile_c(D_MODEL),
        out_shape=jax.ShapeDtypeStruct((bsz, seq, D_MODEL), F32),
        scratch_shapes=[pltpu.VMEM((TM_C, D_MODEL), BF16), pltpu.VMEM((TM_C, D_FF), BF16),
                        pltpu.VMEM((2, CONV_W), F32)],
        compiler_params=params,
        name="mix_ffn2",
    )(x1, y, row(mix_norm), w_gate, conv_w.astype(F32), w_o_attn.astype(BF16),
      w_o_conv.astype(BF16), w_out.astype(BF16), row(ffn2_norm), ffn2_gate.astype(BF16),
      ffn2_up.astype(BF16), ffn2_down.astype(BF16), row(final_norm))
```

```python
import math

import numpy as np
import jax
import jax.numpy as jnp
from jax import lax
from jax.experimental import pallas as pl
from jax.experimental.pallas import tpu as pltpu

F32 = jnp.float32
BF16 = jnp.bfloat16

D_MODEL = 1024
N_HEADS = 8
HEAD_DIM = 64
ATTN_W = N_HEADS * HEAD_DIM
CONV_W = D_MODEL // 2
D_FF = 2816
RMS_EPS = 1e-6
FFN_RES = 0.5
LANES = 128
QKVF_COLS = 3 * ATTN_W + LANES
GATE_COLS = 3 * CONV_W + 2 * D_MODEL
FF_CHUNK = 256
TM_A = 1024
TM_C = 1024
W_IN_ROWS = 256
SUB_A = 2
SUB_C = 4
TQ = 256
TK = TQ
SCORES_AHEAD = 3
LOG2E = 1.4426950408889634
NEG_BIG = -1e30
VT_ROWS = HEAD_DIM + 16
N_SPLIT = 3
BIAS_SLOT = 8
ONES_LANE = N_HEADS
VMEM_LIMIT = 56 * 1024 * 1024


def _rms(x, g):
    inv = lax.rsqrt(jnp.mean(x * x, axis=-1, keepdims=True) + RMS_EPS)
    return (x * inv) * g


def _sub_rows(tm, n_sub):
    th = tm // n_sub
    return [slice(i * th, (i + 1) * th) for i in range(n_sub)]


def _swiglu_act(h_scr, act_scr, wg_ref, wu_ref, subs, fill_h=None):
    for c in range(D_FF // FF_CHUNK):
        sl = slice(c * FF_CHUNK, (c + 1) * FF_CHUNK)
        for rows in subs:
            if c == 0 and fill_h is not None:
                fill_h(rows)
            g = jnp.dot(h_scr[rows, :], wg_ref[:, sl], preferred_element_type=F32)
            u = jnp.dot(h_scr[rows, :], wu_ref[:, sl], preferred_element_type=F32)
            act_scr[rows, sl] = (g * jax.nn.sigmoid(g) * u).astype(BF16)


def _cumsum_rows(x):
    n = x.shape[0]
    row = lax.broadcasted_iota(jnp.int32, x.shape, 0)
    d = 1
    while d < n:
        x = x + jnp.where(row >= d, pltpu.roll(x, d, axis=0), 0.0)
        d *= 2
    return x


def _ffn1_qkv_body(x_ref, g1_ref, wg_ref, wu_ref, wd_ref, g2_ref, wa_ref, bf_ref,
                   x1_ref, qk_ref, vt_ref, cc_ref,
                   h_scr, act_scr, carry_scr):
    subs = _sub_rows(TM_A, SUB_A)

    def fill_h(rows):
        h_scr[rows, :] = _rms(x_ref[rows, :], g1_ref[...]).astype(BF16)
    _swiglu_act(h_scr, act_scr, wg_ref, wu_ref, subs, fill_h)
    for rows in subs:
        x1 = x_ref[rows, :] + FFN_RES * jnp.dot(act_scr[rows, :], wd_ref[...],
                                                preferred_element_type=F32)
        x1_ref[rows, :] = x1
        h_scr[rows, :] = _rms(x1, g2_ref[...]).astype(BF16)

    @pl.when(pl.program_id(1) == 0)
    def _():
        carry_scr[...] = jnp.zeros_like(carry_scr)

    carry = carry_scr[...]
    for rows in subs:
        pr = jnp.dot(h_scr[rows, :], wa_ref[...], preferred_element_type=F32)
        qk_ref[rows, :ATTN_W] = (pr[:, :ATTN_W] * (LOG2E / math.sqrt(HEAD_DIM))).astype(BF16)
        qk_ref[rows, ATTN_W:] = pr[:, ATTN_W:2 * ATTN_W].astype(BF16)
        v_t = pr[:, 2 * ATTN_W:3 * ATTN_W].T.astype(BF16)
        for h in range(N_HEADS):
            vt_ref[h * VT_ROWS:h * VT_ROWS + HEAD_DIM, rows] = v_t[h * HEAD_DIM:(h + 1) * HEAD_DIM, :]
            vt_ref[h * VT_ROWS + HEAD_DIM:(h + 1) * VT_ROWS, rows] = jnp.ones(
                (VT_ROWS - HEAD_DIM, v_t.shape[1]), BF16)
        z = pr[:, 3 * ATTN_W:] + bf_ref[...]
        log_f = jnp.minimum(z, 0.0) - jnp.log1p(jnp.exp(-jnp.abs(z)))
        cum = _cumsum_rows(log_f) + carry
        carry = cum[cum.shape[0] - 1:, :]
        cc_ref[rows, :] = cum * LOG2E
    carry_scr[...] = carry


def _split_cum(c):
    pieces = []
    r = c
    for _ in range(N_SPLIT):
        p = r.astype(BF16)
        pieces.append(p)
        r = r - p.astype(F32)
    lane = lax.broadcasted_iota(jnp.int32, c.shape, 1)
    pieces[0] = jnp.where(lane == ONES_LANE, jnp.ones_like(pieces[0]), pieces[0])
    return jnp.concatenate(pieces, axis=1)


def _bias_selectors():
    sq = np.zeros((N_SPLIT * LANES, ATTN_W), np.float32)
    sk = np.zeros((N_SPLIT * LANES, ATTN_W), np.float32)
    for h in range(N_HEADS):
        base = (h // 2) * LANES + (h % 2) * BIAS_SLOT
        for s in range(N_SPLIT):
            sq[s * LANES + h, base + s] = 1.0
            sq[ONES_LANE, base + N_SPLIT + s] = 1.0
            sk[ONES_LANE, base + s] = 1.0
            sk[s * LANES + h, base + N_SPLIT + s] = -1.0
    return jnp.asarray(sq, BF16), jnp.asarray(sk, BF16)


def _fox_attn_body(qk_ref, vt_ref, cc_ref, sq_ref, sk_ref, y_ref,
                   kf_scr, qc_scr, m_scr, ot_scr, yt_scr):
    n_blk = qk_ref.shape[0] // TQ
    n_pairs = N_HEADS // 2
    lane = lax.broadcasted_iota(jnp.int32, (TK, LANES), 1)

    for r in range(n_blk):
        rows_r = slice(r * TK, (r + 1) * TK)
        pieces = _split_cum(cc_ref[rows_r, :])
        ka = jnp.dot(pieces, sk_ref[...], preferred_element_type=F32).astype(BF16)
        qa = jnp.dot(pieces, sq_ref[...], preferred_element_type=F32).astype(BF16)
        for pair in range(n_pairs):
            cols = slice(pair * LANES, (pair + 1) * LANES)
            kp = qk_ref[rows_r, ATTN_W + pair * LANES:ATTN_W + (pair + 1) * LANES]
            kap = ka[:, cols]
            for hh in range(2):
                own = (lane >= hh * HEAD_DIM) & (lane < (hh + 1) * HEAD_DIM)
                own_b = (lane >= hh * BIAS_SLOT) & (lane < (hh + 1) * BIAS_SLOT)
                rows = slice(hh * TK, (hh + 1) * TK)
                kf_scr[pair, r, rows, :LANES] = jnp.where(own, kp, jnp.zeros_like(kp))
                kf_scr[pair, r, rows, LANES:] = jnp.where(own_b, kap, jnp.zeros_like(kap))
            qc_scr[r, pair, :, :LANES] = qk_ref[rows_r, cols]
            qc_scr[r, pair, :, LANES:] = qa[:, cols]

    k_idx = lax.broadcasted_iota(jnp.int32, (TK, TQ), 0)
    q_idx = lax.broadcasted_iota(jnp.int32, (TK, TQ), 1)

    stream = [(qi, j, pair) for qi in range(n_blk) for j in range(qi + 1)
              for pair in range(n_pairs)]

    pending = {}

    def scores(n):
        qi, j, pair = stream[n]
        pending[n] = lax.dot_general(kf_scr[pair, j], qc_scr[qi, pair],
                                     (((1,), (1,)), ((), ())), preferred_element_type=F32)

    def head(st_pair, par, h, j, on_diagonal):
        st = st_pair[(h % 2) * TK:(h % 2 + 1) * TK, :]
        if on_diagonal:
            st = jnp.where(q_idx >= k_idx, st, NEG_BIG)
        m_old = m_scr[par, h:h + 1, :]
        m_new = jnp.maximum(m_old, jnp.max(st, axis=0, keepdims=True))
        alpha = jnp.exp2(m_old - m_new)
        p = jnp.exp2(st - m_new).astype(BF16)
        m_scr[par, h:h + 1, :] = m_new
        rows = slice(h * VT_ROWS, (h + 1) * VT_ROWS)
        ot_scr[par, rows, :] = alpha * ot_scr[par, rows, :] + jnp.dot(
            vt_ref[rows, j * TK:(j + 1) * TK], p, preferred_element_type=F32)

    for n in range(SCORES_AHEAD):
        scores(n)
    for n, (qi, j, pair) in enumerate(stream):
        par = qi % 2
        if j == 0 and pair == 0:
            m_scr[par] = jnp.full(m_scr.shape[1:], NEG_BIG, F32)
            ot_scr[par] = jnp.zeros(ot_scr.shape[1:], F32)
        st_pair = pending.pop(n)
        head(st_pair, par, 2 * pair, j, j == qi)
        if n + SCORES_AHEAD < len(stream):
            scores(n + SCORES_AHEAD)
        head(st_pair, par, 2 * pair + 1, j, j == qi)
        if j == qi and pair == n_pairs - 1:
            for h in range(N_HEADS):
                o = ot_scr[par, h * VT_ROWS:h * VT_ROWS + HEAD_DIM, :]
                l = ot_scr[par, h * VT_ROWS + HEAD_DIM:h * VT_ROWS + HEAD_DIM + 1, :]
                yt_scr[par, h * HEAD_DIM:(h + 1) * HEAD_DIM, :] = o / l
            y_ref[qi * TQ:(qi + 1) * TQ, :] = yt_scr[par].T.astype(BF16)


def _mix_ffn2_body(x1_ref, y_ref, gm_ref, wgate_ref, cw_ref, woa_ref, woc_ref, wout_ref,
                   g3_ref, wg_ref, wu_ref, wd_ref, gf_ref,
                   o_ref, h_scr, act_scr, tail_scr):
    subs = _sub_rows(TM_C, SUB_C)
    th = TM_C // SUB_C

    def gate(rows, lo, hi):
        return jnp.dot(h_scr[rows, :], wgate_ref[:, lo:hi], preferred_element_type=F32)

    @pl.when(pl.program_id(1) == 0)
    def _():
        tail_scr[...] = jnp.zeros_like(tail_scr)

    row = lax.broadcasted_iota(jnp.int32, (th, CONV_W), 0)
    tail = tail_scr[...]
    mixed = []
    for rows in subs:
        h_scr[rows, :] = _rms(x1_ref[rows, :], gm_ref[...]).astype(BF16)
        c_b = gate(rows, 0, CONV_W)
        u = gate(rows, CONV_W, 2 * CONV_W) * gate(rows, 2 * CONV_W, 3 * CONV_W)
        prev2, prev1 = tail[0:1, :], tail[1:2, :]
        u1 = jnp.where(row == 0, prev1, pltpu.roll(u, 1, axis=0))
        u2 = jnp.where(row == 0, prev2, jnp.where(row == 1, prev1, pltpu.roll(u, 2, axis=0)))
        tail = u[th - 2:th, :]
        conv = cw_ref[0:1, :] * u2 + cw_ref[1:2, :] * u1 + cw_ref[2:3, :] * u
        mixed.append((c_b * conv).astype(BF16))
    tail_scr[...] = tail

    o0 = 3 * CONV_W
    for rows, z in zip(subs, mixed):
        y_conv = jnp.dot(z, woc_ref[...], preferred_element_type=F32)
        y_attn = jnp.dot(y_ref[rows, :], woa_ref[...], preferred_element_type=F32)
        merged = (jax.nn.sigmoid(gate(rows, o0, o0 + D_MODEL)) * y_attn
                  + jax.nn.sigmoid(gate(rows, o0 + D_MODEL, o0 + 2 * D_MODEL)) * y_conv)
        act_scr[rows, :D_MODEL] = merged.astype(BF16)
    for rows in subs:
        x2 = x1_ref[rows, :] + jnp.dot(act_scr[rows, :D_MODEL], wout_ref[...],
                                       preferred_element_type=F32)
        o_ref[rows, :] = x2
        h_scr[rows, :] = _rms(x2, g3_ref[...]).astype(BF16)

    _swiglu_act(h_scr, act_scr, wg_ref, wu_ref, subs)
    for rows in subs:
        x3 = o_ref[rows, :] + FFN_RES * jnp.dot(act_scr[rows, :], wd_ref[...],
                                                preferred_element_type=F32)
        o_ref[rows, :] = _rms(x3, gf_ref[...])


def _split_w_in_body(w_ref, wa_ref, wg_ref):
    n_qkv = 3 * ATTN_W
    wa_ref[:, :n_qkv] = w_ref[:, :n_qkv].astype(BF16)
    tail = w_ref[:, n_qkv:n_qkv + LANES]
    lane = lax.broadcasted_iota(jnp.int32, tail.shape, 1)
    wa_ref[:, n_qkv:] = jnp.where(lane < N_HEADS, tail, 0.0).astype(BF16)
    wg_ref[...] = w_ref[:, n_qkv + N_HEADS:].astype(BF16)


def _resident(shape):
    return pl.BlockSpec(shape, lambda *_: (0,) * len(shape), pipeline_mode=pl.Buffered(1))


def kernel(x, ffn1_norm, ffn1_gate, ffn1_up, ffn1_down, mix_norm, w_in, b_forget, conv_w,
           w_o_attn, w_o_conv, w_out, ffn2_norm, ffn2_gate, ffn2_up, ffn2_down, final_norm):
    bsz, seq, d = x.shape
    assert d == D_MODEL and seq % TM_A == 0 and seq % TM_C == 0 and seq % TQ == 0
    row = lambda v: v.reshape(1, -1).astype(F32)
    w_a, w_gate = pl.pallas_call(
        _split_w_in_body,
        grid=(D_MODEL // W_IN_ROWS,),
        in_specs=[pl.BlockSpec((W_IN_ROWS, w_in.shape[1]), lambda i: (i, 0))],
        out_specs=[pl.BlockSpec((W_IN_ROWS, QKVF_COLS), lambda i: (i, 0)),
                   pl.BlockSpec((W_IN_ROWS, GATE_COLS), lambda i: (i, 0))],
        out_shape=[jax.ShapeDtypeStruct((D_MODEL, QKVF_COLS), BF16),
                   jax.ShapeDtypeStruct((D_MODEL, GATE_COLS), BF16)],
        compiler_params=pltpu.CompilerParams(dimension_semantics=("arbitrary",)),
        name="split_w_in",
    )(w_in)
    b_f = jnp.pad(row(b_forget), ((0, 0), (0, LANES - N_HEADS)))
    sel_q, sel_k = _bias_selectors()
    params = pltpu.CompilerParams(dimension_semantics=("arbitrary", "arbitrary"),
                                  vmem_limit_bytes=VMEM_LIMIT)

    tile_a = lambda w: pl.BlockSpec((None, TM_A, w), lambda b, i: (b, i, 0))
    x1, qk, vt, cum = pl.pallas_call(
        _ffn1_qkv_body,
        grid=(bsz, seq // TM_A),
        in_specs=[tile_a(D_MODEL), _resident((1, D_MODEL)),
                  _resident((D_MODEL, D_FF)), _resident((D_MODEL, D_FF)),
                  _resident((D_FF, D_MODEL)), _resident((1, D_MODEL)),
                  _resident((D_MODEL, QKVF_COLS)), _resident((1, LANES))],
        out_specs=[tile_a(D_MODEL), tile_a(2 * ATTN_W),
                   pl.BlockSpec((None, N_HEADS * VT_ROWS, TM_A), lambda b, i: (b, 0, i)),
                   tile_a(LANES)],
        out_shape=[jax.ShapeDtypeStruct((bsz, seq, D_MODEL), F32),
                   jax.ShapeDtypeStruct((bsz, seq, 2 * ATTN_W), BF16),
                   jax.ShapeDtypeStruct((bsz, N_HEADS * VT_ROWS, seq), BF16),
                   jax.ShapeDtypeStruct((bsz, seq, LANES), F32)],
        scratch_shapes=[pltpu.VMEM((TM_A, D_MODEL), BF16), pltpu.VMEM((TM_A, D_FF), BF16),
                        pltpu.VMEM((1, LANES), F32)],
        compiler_params=params,
        name="ffn1_qkv",
    )(x, row(ffn1_norm), ffn1_gate.astype(BF16), ffn1_up.astype(BF16),
      ffn1_down.astype(BF16), row(mix_norm), w_a, b_f)

    n_blk = seq // TQ
    y = pl.pallas_call(
        _fox_attn_body,
        grid=(bsz,),
        in_specs=[pl.BlockSpec((None, seq, 2 * ATTN_W), lambda b: (b, 0, 0)),
                  pl.BlockSpec((None, N_HEADS * VT_ROWS, seq), lambda b: (b, 0, 0)),
                  pl.BlockSpec((None, seq, LANES), lambda b: (b, 0, 0)),
                  _resident((N_SPLIT * LANES, ATTN_W)), _resident((N_SPLIT * LANES, ATTN_W))],
        out_specs=pl.BlockSpec((None, seq, ATTN_W), lambda b: (b, 0, 0)),
        out_shape=jax.ShapeDtypeStruct((bsz, seq, ATTN_W), BF16),
        scratch_shapes=[pltpu.VMEM((N_HEADS // 2, n_blk, 2 * TK, 2 * LANES), BF16),
                        pltpu.VMEM((n_blk, N_HEADS // 2, TQ, 2 * LANES), BF16),
                        pltpu.VMEM((2, N_HEADS, TQ), F32),
                        pltpu.VMEM((2, N_HEADS * VT_ROWS, TQ), F32),
                        pltpu.VMEM((2, ATTN_W, TQ), F32)],
        compiler_params=pltpu.CompilerParams(dimension_semantics=("arbitrary",),
                                             vmem_limit_bytes=VMEM_LIMIT),
        name="fox_attn",
    )(qk, vt, cum, sel_q, sel_k)

    tile_c = lambda w: pl.BlockSpec((None, TM_C, w), lambda b, i: (b, i, 0))
    return pl.pallas_call(
        _mix_ffn2_body,
        grid=(bsz, seq // TM_C),
        in_specs=[tile_c(D_MODEL), tile_c(ATTN_W), _resident((1, D_MODEL)),
                  _resident((D_MODEL, GATE_COLS)), _resident((3, CONV_W)),
                  _resident((ATTN_W, D_MODEL)), _resident((CONV_W, D_MODEL)),
                  _resident((D_MODEL, D_MODEL)), _resident((1, D_MODEL)),
                  _resident((D_MODEL, D_FF)), _resident((D_MODEL, D_FF)),
                  _resident((D_FF, D_MODEL)), _resident((1, D_MODEL))],
        out_specs=tile_c(D_MODEL),
        out_shape=jax.ShapeDtypeStruct((bsz, seq, D_MODEL), F32),
        scratch_shapes=[pltpu.VMEM((TM_C, D_MODEL), BF16), pltpu.VMEM((TM_C, D_FF), BF16),
                        pltpu.VMEM((2, CONV_W), F32)],
        compiler_params=params,
        name="mix_ffn2",
    )(x1, y, row(mix_norm), w_gate, conv_w.astype(F32), w_o_attn.astype(BF16),
      w_o_conv.astype(BF16), w_out.astype(BF16), row(ffn2_norm), ffn2_gate.astype(BF16),
      ffn2_up.astype(BF16), ffn2_down.astype(BF16), row(final_norm))
```

```python
import math

import numpy as np
import jax
import jax.numpy as jnp
from jax import lax
from jax.experimental import pallas as pl
from jax.experimental.pallas import tpu as pltpu

F32 = jnp.float32
BF16 = jnp.bfloat16

D_MODEL = 1024
N_HEADS = 8
HEAD_DIM = 64
ATTN_W = N_HEADS * HEAD_DIM
CONV_W = D_MODEL // 2
D_FF = 2816
RMS_EPS = 1e-6
FFN_RES = 0.5
LANES = 128
QKVF_COLS = 3 * ATTN_W + LANES
GATE_COLS = 3 * CONV_W + 2 * D_MODEL
FF_CHUNK = 256
TM_A = 1024
TM_C = 1024
W_IN_ROWS = 256
SUB_A = 2
SUB_C = 4
TQ = 256
TK = TQ
SCORES_AHEAD = 3
LOG2E = 1.4426950408889634
NEG_BIG = -1e30
VT_ROWS = HEAD_DIM + 16
N_SPLIT = 3
BIAS_SLOT = 8
ONES_LANE = N_HEADS
VMEM_LIMIT = 56 * 1024 * 1024


def _rms(x, g):
    inv = lax.rsqrt(jnp.mean(x * x, axis=-1, keepdims=True) + RMS_EPS)
    return (x * inv) * g


def _sub_rows(tm, n_sub):
    th = tm // n_sub
    return [slice(i * th, (i + 1) * th) for i in range(n_sub)]


def _swiglu_act(h_scr, act_scr, wg_ref, wu_ref, subs, fill_h=None):
    for c in range(D_FF // FF_CHUNK):
        sl = slice(c * FF_CHUNK, (c + 1) * FF_CHUNK)
        for rows in subs:
            if c == 0 and fill_h is not None:
                fill_h(rows)
            g = jnp.dot(h_scr[rows, :], wg_ref[:, sl], preferred_element_type=F32)
            u = jnp.dot(h_scr[rows, :], wu_ref[:, sl], preferred_element_type=F32)
            act_scr[rows, sl] = (g * jax.nn.sigmoid(g) * u).astype(BF16)


def _cumsum_rows(x):
    n = x.shape[0]
    row = lax.broadcasted_iota(jnp.int32, x.shape, 0)
    d = 1
    while d < n:
        x = x + jnp.where(row >= d, pltpu.roll(x, d, axis=0), 0.0)
        d *= 2
    return x


def _ffn1_qkv_body(x_ref, g1_ref, wg_ref, wu_ref, wd_ref, g2_ref, wa_ref, bf_ref,
                   x1_ref, qk_ref, vt_ref, cc_ref,
                   h_scr, act_scr, carry_scr):
    subs = _sub_rows(TM_A, SUB_A)

    def fill_h(rows):
        h_scr[rows, :] = _rms(x_ref[rows, :], g1_ref[...]).astype(BF16)
    _swiglu_act(h_scr, act_scr, wg_ref, wu_ref, subs, fill_h)
    for rows in subs:
        x1 = x_ref[rows, :] + FFN_RES * jnp.dot(act_scr[rows, :], wd_ref[...],
                                                preferred_element_type=F32)
        x1_ref[rows, :] = x1
        h_scr[rows, :] = _rms(x1, g2_ref[...]).astype(BF16)

    @pl.when(pl.program_id(1) == 0)
    def _():
        carry_scr[...] = jnp.zeros_like(carry_scr)

    carry = carry_scr[...]
    for rows in subs:
        pr = jnp.dot(h_scr[rows, :], wa_ref[...], preferred_element_type=F32)
        qk_ref[rows, :ATTN_W] = (pr[:, :ATTN_W] * (LOG2E / math.sqrt(HEAD_DIM))).astype(BF16)
        qk_ref[rows, ATTN_W:] = pr[:, ATTN_W:2 * ATTN_W].astype(BF16)
        v_t = pr[:, 2 * ATTN_W:3 * ATTN_W].T.astype(BF16)
        for h in range(N_HEADS):
            vt_ref[h * VT_ROWS:h * VT_ROWS + HEAD_DIM, rows] = v_t[h * HEAD_DIM:(h + 1) * HEAD_DIM, :]
            vt_ref[h * VT_ROWS + HEAD_DIM:(h + 1) * VT_ROWS, rows] = jnp.ones(
                (VT_ROWS - HEAD_DIM, v_t.shape[1]), BF16)
        z = pr[:, 3 * ATTN_W:] + bf_ref[...]
        log_f = jnp.minimum(z, 0.0) - jnp.log1p(jnp.exp(-jnp.abs(z)))
        cum = _cumsum_rows(log_f) + carry
        carry = cum[cum.shape[0] - 1:, :]
        cc_ref[rows, :] = cum * LOG2E
    carry_scr[...] = carry


def _split_cum(c):
    pieces = []
    r = c
    for _ in range(N_SPLIT):
        p = r.astype(BF16)
        pieces.append(p)
        r = r - p.astype(F32)
    lane = lax.broadcasted_iota(jnp.int32, c.shape, 1)
    pieces[0] = jnp.where(lane == ONES_LANE, jnp.ones_like(pieces[0]), pieces[0])
    return jnp.concatenate(pieces, axis=1)


def _bias_selectors():
    sq = np.zeros((N_SPLIT * LANES, ATTN_W), np.float32)
    sk = np.zeros((N_SPLIT * LANES, ATTN_W), np.float32)
    for h in range(N_HEADS):
        base = (h // 2) * LANES + (h % 2) * BIAS_SLOT
        for s in range(N_SPLIT):
            sq[s * LANES + h, base + s] = 1.0
            sq[ONES_LANE, base + N_SPLIT + s] = 1.0
            sk[ONES_LANE, base + s] = 1.0
            sk[s * LANES + h, base + N_SPLIT + s] = -1.0
    return jnp.asarray(sq, BF16), jnp.asarray(sk, BF16)


def _fox_attn_body(qk_ref, vt_ref, cc_ref, sq_ref, sk_ref, y_ref,
                   kf_scr, qc_scr, m_scr, ot_scr, yt_scr):
    n_blk = qk_ref.shape[0] // TQ
    n_pairs = N_HEADS // 2
    lane = lax.broadcasted_iota(jnp.int32, (TK, LANES), 1)

    for r in range(n_blk):
        rows_r = slice(r * TK, (r + 1) * TK)
        pieces = _split_cum(cc_ref[rows_r, :])
        ka = jnp.dot(pieces, sk_ref[...], preferred_element_type=F32).astype(BF16)
        qa = jnp.dot(pieces, sq_ref[...], preferred_element_type=F32).astype(BF16)
        for pair in range(n_pairs):
            cols = slice(pair * LANES, (pair + 1) * LANES)
            kp = qk_ref[rows_r, ATTN_W + pair * LANES:ATTN_W + (pair + 1) * LANES]
            kap = ka[:, cols]
            for hh in range(2):
                own = (lane >= hh * HEAD_DIM) & (lane < (hh + 1) * HEAD_DIM)
                own_b = (lane >= hh * BIAS_SLOT) & (lane < (hh + 1) * BIAS_SLOT)
                rows = slice(hh * TK, (hh + 1) * TK)
                kf_scr[pair, r, rows, :LANES] = jnp.where(own, kp, jnp.zeros_like(kp))
                kf_scr[pair, r, rows, LANES:] = jnp.where(own_b, kap, jnp.zeros_like(kap))
            qc_scr[r, pair, :, :LANES] = qk_ref[rows_r, cols]
            qc_scr[r, pair, :, LANES:] = qa[:, cols]

    k_idx = lax.broadcasted_iota(jnp.int32, (TK, LANES), 0)
    q_idx = lax.broadcasted_iota(jnp.int32, (TK, LANES), 1)

    stream = [(qi, j, pair) for qi in range(n_blk) for j in range(qi + 1)
              for pair in range(n_pairs)]

    pending = {}

    def scores(n):
        qi, j, pair = stream[n]
        pending[n] = lax.dot_general(kf_scr[pair, j], qc_scr[qi, pair],
                                     (((1,), (1,)), ((), ())), preferred_element_type=F32)

    def head(st_pair, par, h, j, on_diagonal):
        alphas, ps = [], []
        for c in range(TQ // LANES):
            cols = slice(c * LANES, (c + 1) * LANES)
            st = st_pair[(h % 2) * TK:(h % 2 + 1) * TK, cols]
            if on_diagonal:
                st = jnp.where(q_idx + c * LANES >= k_idx, st, NEG_BIG)
            m_old = m_scr[par, h:h + 1, cols]
            m_new = jnp.maximum(m_old, jnp.max(st, axis=0, keepdims=True))
            alphas.append(jnp.exp2(m_old - m_new))
            ps.append(jnp.exp2(st - m_new).astype(BF16))
            m_scr[par, h:h + 1, cols] = m_new
        rows = slice(h * VT_ROWS, (h + 1) * VT_ROWS)
        pv = jnp.dot(vt_ref[rows, j * TK:(j + 1) * TK], jnp.concatenate(ps, axis=1),
                     preferred_element_type=F32)
        for c in range(TQ // LANES):
            cols = slice(c * LANES, (c + 1) * LANES)
            ot_scr[par, rows, cols] = alphas[c] * ot_scr[par, rows, cols] + pv[:, cols]

    for n in range(SCORES_AHEAD):
        scores(n)
    for n, (qi, j, pair) in enumerate(stream):
        par = qi % 2
        if n + SCORES_AHEAD < len(stream):
            scores(n + SCORES_AHEAD)
        if j == 0 and pair == 0:
            m_scr[par] = jnp.full(m_scr.shape[1:], NEG_BIG, F32)
            ot_scr[par] = jnp.zeros(ot_scr.shape[1:], F32)
        st_pair = pending.pop(n)
        head(st_pair, par, 2 * pair, j, j == qi)
        head(st_pair, par, 2 * pair + 1, j, j == qi)
        if j == qi and pair == n_pairs - 1:
            for h in range(N_HEADS):
                o = ot_scr[par, h * VT_ROWS:h * VT_ROWS + HEAD_DIM, :]
                l = ot_scr[par, h * VT_ROWS + HEAD_DIM:h * VT_ROWS + HEAD_DIM + 1, :]
                yt_scr[par, h * HEAD_DIM:(h + 1) * HEAD_DIM, :] = o / l
            y_ref[qi * TQ:(qi + 1) * TQ, :] = yt_scr[par].T.astype(BF16)


def _mix_ffn2_body(x1_ref, y_ref, gm_ref, wgate_ref, cw_ref, woa_ref, woc_ref, wout_ref,
                   g3_ref, wg_ref, wu_ref, wd_ref, gf_ref,
                   o_ref, h_scr, act_scr, tail_scr):
    subs = _sub_rows(TM_C, SUB_C)
    th = TM_C // SUB_C

    def gate(rows, lo, hi):
        return jnp.dot(h_scr[rows, :], wgate_ref[:, lo:hi], preferred_element_type=F32)

    @pl.when(pl.program_id(1) == 0)
    def _():
        tail_scr[...] = jnp.zeros_like(tail_scr)

    row = lax.broadcasted_iota(jnp.int32, (th, CONV_W), 0)
    tail = tail_scr[...]
    mixed = []
    for rows in subs:
        h_scr[rows, :] = _rms(x1_ref[rows, :], gm_ref[...]).astype(BF16)
        c_b = gate(rows, 0, CONV_W)
        u = gate(rows, CONV_W, 2 * CONV_W) * gate(rows, 2 * CONV_W, 3 * CONV_W)
        prev2, prev1 = tail[0:1, :], tail[1:2, :]
        u1 = jnp.where(row == 0, prev1, pltpu.roll(u, 1, axis=0))
        u2 = jnp.where(row == 0, prev2, jnp.where(row == 1, prev1, pltpu.roll(u, 2, axis=0)))
        tail = u[th - 2:th, :]
        conv = cw_ref[0:1, :] * u2 + cw_ref[1:2, :] * u1 + cw_ref[2:3, :] * u
        mixed.append((c_b * conv).astype(BF16))
    tail_scr[...] = tail

    o0 = 3 * CONV_W
    for rows, z in zip(subs, mixed):
        y_conv = jnp.dot(z, woc_ref[...], preferred_element_type=F32)
        y_attn = jnp.dot(y_ref[rows, :], woa_ref[...], preferred_element_type=F32)
        merged = (jax.nn.sigmoid(gate(rows, o0, o0 + D_MODEL)) * y_attn
                  + jax.nn.sigmoid(gate(rows, o0 + D_MODEL, o0 + 2 * D_MODEL)) * y_conv)
        act_scr[rows, :D_MODEL] = merged.astype(BF16)
    for rows in subs:
        x2 = x1_ref[rows, :] + jnp.dot(act_scr[rows, :D_MODEL], wout_ref[...],
                                       preferred_element_type=F32)
        o_ref[rows, :] = x2
        h_scr[rows, :] = _rms(x2, g3_ref[...]).astype(BF16)

    _swiglu_act(h_scr, act_scr, wg_ref, wu_ref, subs)
    for rows in subs:
        x3 = o_ref[rows, :] + FFN_RES * jnp.dot(act_scr[rows, :], wd_ref[...],
                                                preferred_element_type=F32)
        o_ref[rows, :] = _rms(x3, gf_ref[...])


def _split_w_in_body(w_ref, wa_ref, wg_ref):
    n_qkv = 3 * ATTN_W
    wa_ref[:, :n_qkv] = w_ref[:, :n_qkv].astype(BF16)
    tail = w_ref[:, n_qkv:n_qkv + LANES]
    lane = lax.broadcasted_iota(jnp.int32, tail.shape, 1)
    wa_ref[:, n_qkv:] = jnp.where(lane < N_HEADS, tail, 0.0).astype(BF16)
    wg_ref[...] = w_ref[:, n_qkv + N_HEADS:].astype(BF16)


def _resident(shape):
    return pl.BlockSpec(shape, lambda *_: (0,) * len(shape), pipeline_mode=pl.Buffered(1))


def kernel(x, ffn1_norm, ffn1_gate, ffn1_up, ffn1_down, mix_norm, w_in, b_forget, conv_w,
           w_o_attn, w_o_conv, w_out, ffn2_norm, ffn2_gate, ffn2_up, ffn2_down, final_norm):
    bsz, seq, d = x.shape
    assert d == D_MODEL and seq % TM_A == 0 and seq % TM_C == 0 and seq % TQ == 0
    row = lambda v: v.reshape(1, -1).astype(F32)
    w_a, w_gate = pl.pallas_call(
        _split_w_in_body,
        grid=(D_MODEL // W_IN_ROWS,),
        in_specs=[pl.BlockSpec((W_IN_ROWS, w_in.shape[1]), lambda i: (i, 0))],
        out_specs=[pl.BlockSpec((W_IN_ROWS, QKVF_COLS), lambda i: (i, 0)),
                   pl.BlockSpec((W_IN_ROWS, GATE_COLS), lambda i: (i, 0))],
        out_shape=[jax.ShapeDtypeStruct((D_MODEL, QKVF_COLS), BF16),
                   jax.ShapeDtypeStruct((D_MODEL, GATE_COLS), BF16)],
        compiler_params=pltpu.CompilerParams(dimension_semantics=("arbitrary",)),
        name="split_w_in",
    )(w_in)
    b_f = jnp.pad(row(b_forget), ((0, 0), (0, LANES - N_HEADS)))
    sel_q, sel_k = _bias_selectors()
    params = pltpu.CompilerParams(dimension_semantics=("arbitrary", "arbitrary"),
                                  vmem_limit_bytes=VMEM_LIMIT)

    tile_a = lambda w: pl.BlockSpec((None, TM_A, w), lambda b, i: (b, i, 0))
    x1, qk, vt, cum = pl.pallas_call(
        _ffn1_qkv_body,
        grid=(bsz, seq // TM_A),
        in_specs=[tile_a(D_MODEL), _resident((1, D_MODEL)),
                  _resident((D_MODEL, D_FF)), _resident((D_MODEL, D_FF)),
                  _resident((D_FF, D_MODEL)), _resident((1, D_MODEL)),
                  _resident((D_MODEL, QKVF_COLS)), _resident((1, LANES))],
        out_specs=[tile_a(D_MODEL), tile_a(2 * ATTN_W),
                   pl.BlockSpec((None, N_HEADS * VT_ROWS, TM_A), lambda b, i: (b, 0, i)),
                   tile_a(LANES)],
        out_shape=[jax.ShapeDtypeStruct((bsz, seq, D_MODEL), F32),
                   jax.ShapeDtypeStruct((bsz, seq, 2 * ATTN_W), BF16),
                   jax.ShapeDtypeStruct((bsz, N_HEADS * VT_ROWS, seq), BF16),
                   jax.ShapeDtypeStruct((bsz, seq, LANES), F32)],
        scratch_shapes=[pltpu.VMEM((TM_A, D_MODEL), BF16), pltpu.VMEM((TM_A, D_FF), BF16),
                        pltpu.VMEM((1, LANES), F32)],
        compiler_params=params,
        name="ffn1_qkv",
    )(x, row(ffn1_norm), ffn1_gate.astype(BF16), ffn1_up.astype(BF16),
      ffn1_down.astype(BF16), row(mix_norm), w_a, b_f)

    n_blk = seq // TQ
    y = pl.pallas_call(
        _fox_attn_body,
        grid=(bsz,),
        in_specs=[pl.BlockSpec((None, seq, 2 * ATTN_W), lambda b: (b, 0, 0)),
                  pl.BlockSpec((None, N_HEADS * VT_ROWS, seq), lambda b: (b, 0, 0)),
                  pl.BlockSpec((None, seq, LANES), lambda b: (b, 0, 0)),
                  _resident((N_SPLIT * LANES, ATTN_W)), _resident((N_SPLIT * LANES, ATTN_W))],
        out_specs=pl.BlockSpec((None, seq, ATTN_W), lambda b: (b, 0, 0)),
        out_shape=jax.ShapeDtypeStruct((bsz, seq, ATTN_W), BF16),
        scratch_shapes=[pltpu.VMEM((N_HEADS // 2, n_blk, 2 * TK, 2 * LANES), BF16),
                        pltpu.VMEM((n_blk, N_HEADS // 2, TQ, 2 * LANES), BF16),
                        pltpu.VMEM((2, N_HEADS, TQ), F32),
                        pltpu.VMEM((2, N_HEADS * VT_ROWS, TQ), F32),
                        pltpu.VMEM((2, ATTN_W, TQ), F32)],
        compiler_params=pltpu.CompilerParams(dimension_semantics=("arbitrary",),
                                             vmem_limit_bytes=VMEM_LIMIT),
        name="fox_attn",
    )(qk, vt, cum, sel_q, sel_k)

    tile_c = lambda w: pl.BlockSpec((None, TM_C, w), lambda b, i: (b, i, 0))
    return pl.pallas_call(
        _mix_ffn2_body,
        grid=(bsz, seq // TM_C),
        in_specs=[tile_c(D_MODEL), tile_c(ATTN_W), _resident((1, D_MODEL)),
                  _resident((D_MODEL, GATE_COLS)), _resident((3, CONV_W)),
                  _resident((ATTN_W, D_MODEL)), _resident((CONV_W, D_MODEL)),
                  _resident((D_MODEL, D_MODEL)), _resident((1, D_MODEL)),
                  _resident((D_MODEL, D_FF)), _resident((D_MODEL, D_FF)),
                  _resident((D_FF, D_MODEL)), _resident((1, D_MODEL))],
        out_specs=tile_c(D_MODEL),
        out_shape=jax.ShapeDtypeStruct((bsz, seq, D_MODEL), F32),
        scratch_shapes=[pltpu.VMEM((TM_C, D_MODEL), BF16), pltpu.VMEM((TM_C, D_FF), BF16),
                        pltpu.VMEM((2, CONV_W), F32)],
        compiler_params=params,
        name="mix_ffn2",
    )(x1, y, row(mix_norm), w_gate, conv_w.astype(F32), w_o_attn.astype(BF16),
      w_o_conv.astype(BF16), w_out.astype(BF16), row(ffn2_norm), ffn2_gate.astype(BF16),
      ffn2_up.astype(BF16), ffn2_down.astype(BF16), row(final_norm))
```

```python
import math

import numpy as np
import jax
import jax.numpy as jnp
from jax import lax
from jax.experimental import pallas as pl
from jax.experimental.pallas import tpu as pltpu

F32 = jnp.float32
BF16 = jnp.bfloat16

D_MODEL = 1024
N_HEADS = 8
HEAD_DIM = 64
ATTN_W = N_HEADS * HEAD_DIM
CONV_W = D_MODEL // 2
D_FF = 2816
RMS_EPS = 1e-6
FFN_RES = 0.5
LANES = 128
QKVF_COLS = 3 * ATTN_W + LANES
GATE_COLS = 3 * CONV_W + 2 * D_MODEL
FF_CHUNK = 256
TM_A = 1024
TM_C = 1024
W_IN_ROWS = 256
SUB_A = 2
SUB_C = 4
TQ = 256
TK = TQ
SCORES_AHEAD = 3
LOG2E = 1.4426950408889634
NEG_BIG = -1e30
VT_ROWS = HEAD_DIM + 16
N_SPLIT = 3
BIAS_SLOT = 8
ONES_LANE = N_HEADS
VMEM_LIMIT = 56 * 1024 * 1024


def _rms(x, g):
    inv = lax.rsqrt(jnp.mean(x * x, axis=-1, keepdims=True) + RMS_EPS)
    return (x * inv) * g


def _sub_rows(tm, n_sub):
    th = tm // n_sub
    return [slice(i * th, (i + 1) * th) for i in range(n_sub)]


def _swiglu_act(h_scr, act_scr, wg_ref, wu_ref, subs, fill_h=None):
    for c in range(D_FF // FF_CHUNK):
        sl = slice(c * FF_CHUNK, (c + 1) * FF_CHUNK)
        for rows in subs:
            if c == 0 and fill_h is not None:
                fill_h(rows)
            g = jnp.dot(h_scr[rows, :], wg_ref[:, sl], preferred_element_type=F32)
            u = jnp.dot(h_scr[rows, :], wu_ref[:, sl], preferred_element_type=F32)
            act_scr[rows, sl] = (g * jax.nn.sigmoid(g) * u).astype(BF16)


def _cumsum_rows(x):
    n = x.shape[0]
    row = lax.broadcasted_iota(jnp.int32, x.shape, 0)
    d = 1
    while d < n:
        x = x + jnp.where(row >= d, pltpu.roll(x, d, axis=0), 0.0)
        d *= 2
    return x


def _ffn1_qkv_body(x_ref, g1_ref, wg_ref, wu_ref, wd_ref, g2_ref, wa_ref, bf_ref,
                   x1_ref, qk_ref, vt_ref, cc_ref,
                   h_scr, act_scr, carry_scr):
    subs = _sub_rows(TM_A, SUB_A)

    def fill_h(rows):
        h_scr[rows, :] = _rms(x_ref[rows, :], g1_ref[...]).astype(BF16)
    _swiglu_act(h_scr, act_scr, wg_ref, wu_ref, subs, fill_h)
    for rows in subs:
        x1 = x_ref[rows, :] + FFN_RES * jnp.dot(act_scr[rows, :], wd_ref[...],
                                                preferred_element_type=F32)
        x1_ref[rows, :] = x1
        h_scr[rows, :] = _rms(x1, g2_ref[...]).astype(BF16)

    @pl.when(pl.program_id(1) == 0)
    def _():
        carry_scr[...] = jnp.zeros_like(carry_scr)

    carry = carry_scr[...]
    for rows in subs:
        pr = jnp.dot(h_scr[rows, :], wa_ref[...], preferred_element_type=F32)
        qk_ref[rows, :ATTN_W] = (pr[:, :ATTN_W] * (LOG2E / math.sqrt(HEAD_DIM))).astype(BF16)
        qk_ref[rows, ATTN_W:] = pr[:, ATTN_W:2 * ATTN_W].astype(BF16)
        v_t = pr[:, 2 * ATTN_W:3 * ATTN_W].T.astype(BF16)
        for h in range(N_HEADS):
            vt_ref[h * VT_ROWS:h * VT_ROWS + HEAD_DIM, rows] = v_t[h * HEAD_DIM:(h + 1) * HEAD_DIM, :]
            vt_ref[h * VT_ROWS + HEAD_DIM:(h + 1) * VT_ROWS, rows] = jnp.ones(
                (VT_ROWS - HEAD_DIM, v_t.shape[1]), BF16)
        z = pr[:, 3 * ATTN_W:] + bf_ref[...]
        log_f = jnp.minimum(z, 0.0) - jnp.log1p(jnp.exp(-jnp.abs(z)))
        cum = _cumsum_rows(log_f) + carry
        carry = cum[cum.shape[0] - 1:, :]
        cc_ref[rows, :] = cum * LOG2E
    carry_scr[...] = carry


def _split_cum(c):
    pieces = []
    r = c
    for _ in range(N_SPLIT):
        p = r.astype(BF16)
        pieces.append(p)
        r = r - p.astype(F32)
    lane = lax.broadcasted_iota(jnp.int32, c.shape, 1)
    pieces[0] = jnp.where(lane == ONES_LANE, jnp.ones_like(pieces[0]), pieces[0])
    return jnp.concatenate(pieces, axis=1)


def _bias_selectors():
    sq = np.zeros((N_SPLIT * LANES, ATTN_W), np.float32)
    sk = np.zeros((N_SPLIT * LANES, ATTN_W), np.float32)
    for h in range(N_HEADS):
        base = (h // 2) * LANES + (h % 2) * BIAS_SLOT
        for s in range(N_SPLIT):
            sq[s * LANES + h, base + s] = 1.0
            sq[ONES_LANE, base + N_SPLIT + s] = 1.0
            sk[ONES_LANE, base + s] = 1.0
            sk[s * LANES + h, base + N_SPLIT + s] = -1.0
    return jnp.asarray(sq, BF16), jnp.asarray(sk, BF16)


def _fox_attn_body(qk_ref, vt_ref, cc_ref, sq_ref, sk_ref, y_ref,
                   kf_scr, qc_scr, m_scr, ot_scr, yt_scr):
    n_blk = qk_ref.shape[0] // TQ
    n_pairs = N_HEADS // 2
    lane = lax.broadcasted_iota(jnp.int32, (TK, LANES), 1)

    for r in range(n_blk):
        rows_r = slice(r * TK, (r + 1) * TK)
        pieces = _split_cum(cc_ref[rows_r, :])
        ka = jnp.dot(pieces, sk_ref[...], preferred_element_type=F32).astype(BF16)
        qa = jnp.dot(pieces, sq_ref[...], preferred_element_type=F32).astype(BF16)
        for pair in range(n_pairs):
            cols = slice(pair * LANES, (pair + 1) * LANES)
            kp = qk_ref[rows_r, ATTN_W + pair * LANES:ATTN_W + (pair + 1) * LANES]
            kap = ka[:, cols]
            for hh in range(2):
                own = (lane >= hh * HEAD_DIM) & (lane < (hh + 1) * HEAD_DIM)
                own_b = (lane >= hh * BIAS_SLOT) & (lane < (hh + 1) * BIAS_SLOT)
                rows = slice(hh * TK, (hh + 1) * TK)
                kf_scr[pair, r, rows, :LANES] = jnp.where(own, kp, jnp.zeros_like(kp))
                kf_scr[pair, r, rows, LANES:] = jnp.where(own_b, kap, jnp.zeros_like(kap))
            qc_scr[r, pair, :, :LANES] = qk_ref[rows_r, cols]
            qc_scr[r, pair, :, LANES:] = qa[:, cols]

    tri = (lax.broadcasted_iota(jnp.int32, (LANES, LANES), 1)
           >= lax.broadcasted_iota(jnp.int32, (LANES, LANES), 0))

    stream = [(qi, j, pair) for qi in range(n_blk) for j in range(qi + 1)
              for pair in range(n_pairs)]

    pending = {}

    def scores(n):
        qi, j, pair = stream[n]
        pending[n] = lax.dot_general(kf_scr[pair, j], qc_scr[qi, pair],
                                     (((1,), (1,)), ((), ())), preferred_element_type=F32)

    def head(st_pair, par, h, j, on_diagonal):
        alphas, ps = [], []
        base = (h % 2) * TK
        n_row_groups = TK // LANES
        for c in range(TQ // LANES):
            cols = slice(c * LANES, (c + 1) * LANES)
            n_live = c + 1 if on_diagonal else n_row_groups
            parts = []
            for r in range(n_live):
                blk = st_pair[base + r * LANES:base + (r + 1) * LANES, cols]
                if on_diagonal and r == c:
                    blk = jnp.where(tri, blk, NEG_BIG)
                parts.append(blk)
            m_old = m_scr[par, h:h + 1, cols]
            m_new = m_old
            for blk in parts:
                m_new = jnp.maximum(m_new, jnp.max(blk, axis=0, keepdims=True))
            alphas.append(jnp.exp2(m_old - m_new))
            p_parts = [jnp.exp2(blk - m_new).astype(BF16) for blk in parts]
            p_parts += [jnp.zeros((LANES, LANES), BF16)] * (n_row_groups - n_live)
            ps.append(jnp.concatenate(p_parts, axis=0))
            m_scr[par, h:h + 1, cols] = m_new
        rows = slice(h * VT_ROWS, (h + 1) * VT_ROWS)
        pv = jnp.dot(vt_ref[rows, j * TK:(j + 1) * TK], jnp.concatenate(ps, axis=1),
                     preferred_element_type=F32)
        for c in range(TQ // LANES):
            cols = slice(c * LANES, (c + 1) * LANES)
            ot_scr[par, rows, cols] = alphas[c] * ot_scr[par, rows, cols] + pv[:, cols]

    for n in range(SCORES_AHEAD):
        scores(n)
    for n, (qi, j, pair) in enumerate(stream):
        par = qi % 2
        if n + SCORES_AHEAD < len(stream):
            scores(n + SCORES_AHEAD)
        if j == 0 and pair == 0:
            m_scr[par] = jnp.full(m_scr.shape[1:], NEG_BIG, F32)
            ot_scr[par] = jnp.zeros(ot_scr.shape[1:], F32)
        st_pair = pending.pop(n)
        head(st_pair, par, 2 * pair, j, j == qi)
        head(st_pair, par, 2 * pair + 1, j, j == qi)
        if j == qi and pair == n_pairs - 1:
            for h in range(N_HEADS):
                o = ot_scr[par, h * VT_ROWS:h * VT_ROWS + HEAD_DIM, :]
                l = ot_scr[par, h * VT_ROWS + HEAD_DIM:h * VT_ROWS + HEAD_DIM + 1, :]
                yt_scr[par, h * HEAD_DIM:(h + 1) * HEAD_DIM, :] = o / l
            y_ref[qi * TQ:(qi + 1) * TQ, :] = yt_scr[par].T.astype(BF16)


def _mix_ffn2_body(x1_ref, y_ref, gm_ref, wgate_ref, cw_ref, woa_ref, woc_ref, wout_ref,
                   g3_ref, wg_ref, wu_ref, wd_ref, gf_ref,
                   o_ref, h_scr, act_scr, tail_scr):
    subs = _sub_rows(TM_C, SUB_C)
    th = TM_C // SUB_C

    def gate(rows, lo, hi):
        return jnp.dot(h_scr[rows, :], wgate_ref[:, lo:hi], preferred_element_type=F32)

    @pl.when(pl.program_id(1) == 0)
    def _():
        tail_scr[...] = jnp.zeros_like(tail_scr)

    row = lax.broadcasted_iota(jnp.int32, (th, CONV_W), 0)
    tail = tail_scr[...]
    mixed = []
    for rows in subs:
        h_scr[rows, :] = _rms(x1_ref[rows, :], gm_ref[...]).astype(BF16)
        c_b = gate(rows, 0, CONV_W)
        u = gate(rows, CONV_W, 2 * CONV_W) * gate(rows, 2 * CONV_W, 3 * CONV_W)
        prev2, prev1 = tail[0:1, :], tail[1:2, :]
        u1 = jnp.where(row == 0, prev1, pltpu.roll(u, 1, axis=0))
        u2 = jnp.where(row == 0, prev2, jnp.where(row == 1, prev1, pltpu.roll(u, 2, axis=0)))
        tail = u[th - 2:th, :]
        conv = cw_ref[0:1, :] * u2 + cw_ref[1:2, :] * u1 + cw_ref[2:3, :] * u
        mixed.append((c_b * conv).astype(BF16))
    tail_scr[...] = tail

    o0 = 3 * CONV_W
    for rows, z in zip(subs, mixed):
        y_conv = jnp.dot(z, woc_ref[...], preferred_element_type=F32)
        y_attn = jnp.dot(y_ref[rows, :], woa_ref[...], preferred_element_type=F32)
        merged = (jax.nn.sigmoid(gate(rows, o0, o0 + D_MODEL)) * y_attn
                  + jax.nn.sigmoid(gate(rows, o0 + D_MODEL, o0 + 2 * D_MODEL)) * y_conv)
        act_scr[rows, :D_MODEL] = merged.astype(BF16)
    for rows in subs:
        x2 = x1_ref[rows, :] + jnp.dot(act_scr[rows, :D_MODEL], wout_ref[...],
                                       preferred_element_type=F32)
        o_ref[rows, :] = x2
        h_scr[rows, :] = _rms(x2, g3_ref[...]).astype(BF16)

    _swiglu_act(h_scr, act_scr, wg_ref, wu_ref, subs)
    for rows in subs:
        x3 = o_ref[rows, :] + FFN_RES * jnp.dot(act_scr[rows, :], wd_ref[...],
                                                preferred_element_type=F32)
        o_ref[rows, :] = _rms(x3, gf_ref[...])


def _split_w_in_body(w_ref, wa_ref, wg_ref):
    n_qkv = 3 * ATTN_W
    wa_ref[:, :n_qkv] = w_ref[:, :n_qkv].astype(BF16)
    tail = w_ref[:, n_qkv:n_qkv + LANES]
    lane = lax.broadcasted_iota(jnp.int32, tail.shape, 1)
    wa_ref[:, n_qkv:] = jnp.where(lane < N_HEADS, tail, 0.0).astype(BF16)
    wg_ref[...] = w_ref[:, n_qkv + N_HEADS:].astype(BF16)


def _resident(shape):
    return pl.BlockSpec(shape, lambda *_: (0,) * len(shape), pipeline_mode=pl.Buffered(1))


def kernel(x, ffn1_norm, ffn1_gate, ffn1_up, ffn1_down, mix_norm, w_in, b_forget, conv_w,
           w_o_attn, w_o_conv, w_out, ffn2_norm, ffn2_gate, ffn2_up, ffn2_down, final_norm):
    bsz, seq, d = x.shape
    assert d == D_MODEL and seq % TM_A == 0 and seq % TM_C == 0 and seq % TQ == 0
    row = lambda v: v.reshape(1, -1).astype(F32)
    w_a, w_gate = pl.pallas_call(
        _split_w_in_body,
        grid=(D_MODEL // W_IN_ROWS,),
        in_specs=[pl.BlockSpec((W_IN_ROWS, w_in.shape[1]), lambda i: (i, 0))],
        out_specs=[pl.BlockSpec((W_IN_ROWS, QKVF_COLS), lambda i: (i, 0)),
                   pl.BlockSpec((W_IN_ROWS, GATE_COLS), lambda i: (i, 0))],
        out_shape=[jax.ShapeDtypeStruct((D_MODEL, QKVF_COLS), BF16),
                   jax.ShapeDtypeStruct((D_MODEL, GATE_COLS), BF16)],
        compiler_params=pltpu.CompilerParams(dimension_semantics=("arbitrary",)),
        name="split_w_in",
    )(w_in)
    b_f = jnp.pad(row(b_forget), ((0, 0), (0, LANES - N_HEADS)))
    sel_q, sel_k = _bias_selectors()
    params = pltpu.CompilerParams(dimension_semantics=("arbitrary", "arbitrary"),
                                  vmem_limit_bytes=VMEM_LIMIT)

    tile_a = lambda w: pl.BlockSpec((None, TM_A, w), lambda b, i: (b, i, 0))
    x1, qk, vt, cum = pl.pallas_call(
        _ffn1_qkv_body,
        grid=(bsz, seq // TM_A),
        in_specs=[tile_a(D_MODEL), _resident((1, D_MODEL)),
                  _resident((D_MODEL, D_FF)), _resident((D_MODEL, D_FF)),
                  _resident((D_FF, D_MODEL)), _resident((1, D_MODEL)),
                  _resident((D_MODEL, QKVF_COLS)), _resident((1, LANES))],
        out_specs=[tile_a(D_MODEL), tile_a(2 * ATTN_W),
                   pl.BlockSpec((None, N_HEADS * VT_ROWS, TM_A), lambda b, i: (b, 0, i)),
                   tile_a(LANES)],
        out_shape=[jax.ShapeDtypeStruct((bsz, seq, D_MODEL), F32),
                   jax.ShapeDtypeStruct((bsz, seq, 2 * ATTN_W), BF16),
                   jax.ShapeDtypeStruct((bsz, N_HEADS * VT_ROWS, seq), BF16),
                   jax.ShapeDtypeStruct((bsz, seq, LANES), F32)],
        scratch_shapes=[pltpu.VMEM((TM_A, D_MODEL), BF16), pltpu.VMEM((TM_A, D_FF), BF16),
                        pltpu.VMEM((1, LANES), F32)],
        compiler_params=params,
        name="ffn1_qkv",
    )(x, row(ffn1_norm), ffn1_gate.astype(BF16), ffn1_up.astype(BF16),
      ffn1_down.astype(BF16), row(mix_norm), w_a, b_f)

    n_blk = seq // TQ
    y = pl.pallas_call(
        _fox_attn_body,
        grid=(bsz,),
        in_specs=[pl.BlockSpec((None, seq, 2 * ATTN_W), lambda b: (b, 0, 0)),
                  pl.BlockSpec((None, N_HEADS * VT_ROWS, seq), lambda b: (b, 0, 0)),
                  pl.BlockSpec((None, seq, LANES), lambda b: (b, 0, 0)),
                  _resident((N_SPLIT * LANES, ATTN_W)), _resident((N_SPLIT * LANES, ATTN_W))],
        out_specs=pl.BlockSpec((None, seq, ATTN_W), lambda b: (b, 0, 0)),
        out_shape=jax.ShapeDtypeStruct((bsz, seq, ATTN_W), BF16),
        scratch_shapes=[pltpu.VMEM((N_HEADS // 2, n_blk, 2 * TK, 2 * LANES), BF16),
                        pltpu.VMEM((n_blk, N_HEADS // 2, TQ, 2 * LANES), BF16),
                        pltpu.VMEM((2, N_HEADS, TQ), F32),
                        pltpu.VMEM((2, N_HEADS * VT_ROWS, TQ), F32),
                        pltpu.VMEM((2, ATTN_W, TQ), F32)],
        compiler_params=pltpu.CompilerParams(dimension_semantics=("arbitrary",),
                                             vmem_limit_bytes=VMEM_LIMIT),
        name="fox_attn",
    )(qk, vt, cum, sel_q, sel_k)

    tile_c = lambda w: pl.BlockSpec((None, TM_C, w), lambda b, i: (b, i, 0))
    return pl.pallas_call(
        _mix_ffn2_body,
        grid=(bsz, seq // TM_C),
        in_specs=[tile_c(D_MODEL), tile_c(ATTN_W), _resident((1, D_MODEL)),
                  _resident((D_MODEL, GATE_COLS)), _resident((3, CONV_W)),
                  _resident((ATTN_W, D_MODEL)), _resident((CONV_W, D_MODEL)),
                  _resident((D_MODEL, D_MODEL)), _resident((1, D_MODEL)),
                  _resident((D_MODEL, D_FF)), _resident((D_MODEL, D_FF)),
                  _resident((D_FF, D_MODEL)), _resident((1, D_MODEL))],
        out_specs=tile_c(D_MODEL),
        out_shape=jax.ShapeDtypeStruct((bsz, seq, D_MODEL), F32),
        scratch_shapes=[pltpu.VMEM((TM_C, D_MODEL), BF16), pltpu.VMEM((TM_C, D_FF), BF16),
                        pltpu.VMEM((2, CONV_W), F32)],
        compiler_params=params,
        name="mix_ffn2",
    )(x1, y, row(mix_norm), w_gate, conv_w.astype(F32), w_o_attn.astype(BF16),
      w_o_conv.astype(BF16), w_out.astype(BF16), row(ffn2_norm), ffn2_gate.astype(BF16),
      ffn2_up.astype(BF16), ffn2_down.astype(BF16), row(final_norm))
```

```python
import math

import numpy as np
import jax
import jax.numpy as jnp
from jax import lax
from jax.experimental import pallas as pl
from jax.experimental.pallas import tpu as pltpu

F32 = jnp.float32
BF16 = jnp.bfloat16

D_MODEL = 1024
N_HEADS = 8
HEAD_DIM = 64
ATTN_W = N_HEADS * HEAD_DIM
CONV_W = D_MODEL // 2
D_FF = 2816
RMS_EPS = 1e-6
FFN_RES = 0.5
LANES = 128
QKVF_COLS = 3 * ATTN_W + LANES
GATE_COLS = 3 * CONV_W + 2 * D_MODEL
FF_CHUNK = 256
TM_A = 1024
TM_C = 1024
W_IN_ROWS = 256
SUB_A = 2
SUB_C = 4
TQ = 256
TK = TQ
SCORES_AHEAD = 4
LOG2E = 1.4426950408889634
NEG_BIG = -1e30
VT_ROWS = HEAD_DIM + 16
N_SPLIT = 3
BIAS_SLOT = 8
ONES_LANE = N_HEADS
VMEM_LIMIT = 56 * 1024 * 1024


def _rms(x, g):
    inv = lax.rsqrt(jnp.mean(x * x, axis=-1, keepdims=True) + RMS_EPS)
    return (x * inv) * g


def _sub_rows(tm, n_sub):
    th = tm // n_sub
    return [slice(i * th, (i + 1) * th) for i in range(n_sub)]


def _swiglu_act(h_scr, act_scr, wg_ref, wu_ref, subs, fill_h=None):
    for c in range(D_FF // FF_CHUNK):
        sl = slice(c * FF_CHUNK, (c + 1) * FF_CHUNK)
        for rows in subs:
            if c == 0 and fill_h is not None:
                fill_h(rows)
            g = jnp.dot(h_scr[rows, :], wg_ref[:, sl], preferred_element_type=F32)
            u = jnp.dot(h_scr[rows, :], wu_ref[:, sl], preferred_element_type=F32)
            act_scr[rows, sl] = (g * jax.nn.sigmoid(g) * u).astype(BF16)


def _cumsum_rows(x):
    n = x.shape[0]
    row = lax.broadcasted_iota(jnp.int32, x.shape, 0)
    d = 1
    while d < n:
        x = x + jnp.where(row >= d, pltpu.roll(x, d, axis=0), 0.0)
        d *= 2
    return x


def _ffn1_qkv_body(x_ref, g1_ref, wg_ref, wu_ref, wd_ref, g2_ref, wa_ref, bf_ref,
                   x1_ref, qk_ref, vt_ref, cc_ref,
                   h_scr, act_scr, carry_scr):
    subs = _sub_rows(TM_A, SUB_A)

    def fill_h(rows):
        h_scr[rows, :] = _rms(x_ref[rows, :], g1_ref[...]).astype(BF16)
    _swiglu_act(h_scr, act_scr, wg_ref, wu_ref, subs, fill_h)
    for rows in subs:
        x1 = x_ref[rows, :] + FFN_RES * jnp.dot(act_scr[rows, :], wd_ref[...],
                                                preferred_element_type=F32)
        x1_ref[rows, :] = x1
        h_scr[rows, :] = _rms(x1, g2_ref[...]).astype(BF16)

    @pl.when(pl.program_id(1) == 0)
    def _():
        carry_scr[...] = jnp.zeros_like(carry_scr)

    carry = carry_scr[...]
    for rows in subs:
        pr = jnp.dot(h_scr[rows, :], wa_ref[...], preferred_element_type=F32)
        qk_ref[rows, :ATTN_W] = (pr[:, :ATTN_W] * (LOG2E / math.sqrt(HEAD_DIM))).astype(BF16)
        qk_ref[rows, ATTN_W:] = pr[:, ATTN_W:2 * ATTN_W].astype(BF16)
        v_t = pr[:, 2 * ATTN_W:3 * ATTN_W].T.astype(BF16)
        for h in range(N_HEADS):
            vt_ref[h * VT_ROWS:h * VT_ROWS + HEAD_DIM, rows] = v_t[h * HEAD_DIM:(h + 1) * HEAD_DIM, :]
            vt_ref[h * VT_ROWS + HEAD_DIM:(h + 1) * VT_ROWS, rows] = jnp.ones(
                (VT_ROWS - HEAD_DIM, v_t.shape[1]), BF16)
        z = pr[:, 3 * ATTN_W:] + bf_ref[...]
        log_f = jnp.minimum(z, 0.0) - jnp.log1p(jnp.exp(-jnp.abs(z)))
        cum = _cumsum_rows(log_f) + carry
        carry = cum[cum.shape[0] - 1:, :]
        cc_ref[rows, :] = cum * LOG2E
    carry_scr[...] = carry


def _split_cum(c):
    pieces = []
    r = c
    for _ in range(N_SPLIT):
        p = r.astype(BF16)
        pieces.append(p)
        r = r - p.astype(F32)
    lane = lax.broadcasted_iota(jnp.int32, c.shape, 1)
    pieces[0] = jnp.where(lane == ONES_LANE, jnp.ones_like(pieces[0]), pieces[0])
    return jnp.concatenate(pieces, axis=1)


def _bias_selectors():
    sq = np.zeros((N_SPLIT * LANES, ATTN_W), np.float32)
    sk = np.zeros((N_SPLIT * LANES, ATTN_W), np.float32)
    for h in range(N_HEADS):
        base = (h // 2) * LANES + (h % 2) * BIAS_SLOT
        for s in range(N_SPLIT):
            sq[s * LANES + h, base + s] = 1.0
            sq[ONES_LANE, base + N_SPLIT + s] = 1.0
            sk[ONES_LANE, base + s] = 1.0
            sk[s * LANES + h, base + N_SPLIT + s] = -1.0
    return jnp.asarray(sq, BF16), jnp.asarray(sk, BF16)


def _fox_attn_body(qk_ref, vt_ref, cc_ref, sq_ref, sk_ref, y_ref,
                   kf_scr, qc_scr, m_scr, ot_scr, yt_scr):
    n_blk = qk_ref.shape[0] // TQ
    n_pairs = N_HEADS // 2
    lane = lax.broadcasted_iota(jnp.int32, (TK, LANES), 1)

    for r in range(n_blk):
        rows_r = slice(r * TK, (r + 1) * TK)
        pieces = _split_cum(cc_ref[rows_r, :])
        ka = jnp.dot(pieces, sk_ref[...], preferred_element_type=F32).astype(BF16)
        qa = jnp.dot(pieces, sq_ref[...], preferred_element_type=F32).astype(BF16)
        for pair in range(n_pairs):
            cols = slice(pair * LANES, (pair + 1) * LANES)
            kp = qk_ref[rows_r, ATTN_W + pair * LANES:ATTN_W + (pair + 1) * LANES]
            kap = ka[:, cols]
            for hh in range(2):
                own = (lane >= hh * HEAD_DIM) & (lane < (hh + 1) * HEAD_DIM)
                own_b = (lane >= hh * BIAS_SLOT) & (lane < (hh + 1) * BIAS_SLOT)
                rows = slice(hh * TK, (hh + 1) * TK)
                kf_scr[pair, r, rows, :LANES] = jnp.where(own, kp, jnp.zeros_like(kp))
                kf_scr[pair, r, rows, LANES:] = jnp.where(own_b, kap, jnp.zeros_like(kap))
            qc_scr[r, pair, :, :LANES] = qk_ref[rows_r, cols]
            qc_scr[r, pair, :, LANES:] = qa[:, cols]

    k_idx = lax.broadcasted_iota(jnp.int32, (TK, LANES), 0)
    q_idx = lax.broadcasted_iota(jnp.int32, (TK, LANES), 1)

    stream = [(qi, j, pair) for qi in range(n_blk) for j in range(qi + 1)
              for pair in range(n_pairs)]

    pending = {}

    def scores(n):
        qi, j, pair = stream[n]
        pending[n] = lax.dot_general(kf_scr[pair, j], qc_scr[qi, pair],
                                     (((1,), (1,)), ((), ())), preferred_element_type=F32)

    def head(st_pair, par, h, j, on_diagonal):
        alphas, ps = [], []
        for c in range(TQ // LANES):
            cols = slice(c * LANES, (c + 1) * LANES)
            st = st_pair[(h % 2) * TK:(h % 2 + 1) * TK, cols]
            if on_diagonal:
                st = jnp.where(q_idx + c * LANES >= k_idx, st, NEG_BIG)
            m_old = m_scr[par, h:h + 1, cols]
            m_new = jnp.maximum(m_old, jnp.max(st, axis=0, keepdims=True))
            alphas.append(jnp.exp2(m_old - m_new))
            ps.append(jnp.exp2(st - m_new).astype(BF16))
            m_scr[par, h:h + 1, cols] = m_new
        rows = slice(h * VT_ROWS, (h + 1) * VT_ROWS)
        pv = jnp.dot(vt_ref[rows, j * TK:(j + 1) * TK], jnp.concatenate(ps, axis=1),
                     preferred_element_type=F32)
        for c in range(TQ // LANES):
            cols = slice(c * LANES, (c + 1) * LANES)
            ot_scr[par, rows, cols] = alphas[c] * ot_scr[par, rows, cols] + pv[:, cols]

    for n in range(SCORES_AHEAD):
        scores(n)
    for n, (qi, j, pair) in enumerate(stream):
        par = qi % 2
        if n + SCORES_AHEAD < len(stream):
            scores(n + SCORES_AHEAD)
        if j == 0 and pair == 0:
            m_scr[par] = jnp.full(m_scr.shape[1:], NEG_BIG, F32)
            ot_scr[par] = jnp.zeros(ot_scr.shape[1:], F32)
        st_pair = pending.pop(n)
        head(st_pair, par, 2 * pair, j, j == qi)
        head(st_pair, par, 2 * pair + 1, j, j == qi)
        if j == qi and pair == n_pairs - 1:
            for h in range(N_HEADS):
                o = ot_scr[par, h * VT_ROWS:h * VT_ROWS + HEAD_DIM, :]
                l = ot_scr[par, h * VT_ROWS + HEAD_DIM:h * VT_ROWS + HEAD_DIM + 1, :]
                yt_scr[par, h * HEAD_DIM:(h + 1) * HEAD_DIM, :] = o / l
            y_ref[qi * TQ:(qi + 1) * TQ, :] = yt_scr[par].T.astype(BF16)


def _mix_ffn2_body(x1_ref, y_ref, gm_ref, wgate_ref, cw_ref, woa_ref, woc_ref, wout_ref,
                   g3_ref, wg_ref, wu_ref, wd_ref, gf_ref,
                   o_ref, h_scr, act_scr, tail_scr):
    subs = _sub_rows(TM_C, SUB_C)
    th = TM_C // SUB_C

    def gate(rows, lo, hi):
        return jnp.dot(h_scr[rows, :], wgate_ref[:, lo:hi], preferred_element_type=F32)

    @pl.when(pl.program_id(1) == 0)
    def _():
        tail_scr[...] = jnp.zeros_like(tail_scr)

    row = lax.broadcasted_iota(jnp.int32, (th, CONV_W), 0)
    tail = tail_scr[...]
    mixed = []
    for rows in subs:
        h_scr[rows, :] = _rms(x1_ref[rows, :], gm_ref[...]).astype(BF16)
        c_b = gate(rows, 0, CONV_W)
        u = gate(rows, CONV_W, 2 * CONV_W) * gate(rows, 2 * CONV_W, 3 * CONV_W)
        prev2, prev1 = tail[0:1, :], tail[1:2, :]
        u1 = jnp.where(row == 0, prev1, pltpu.roll(u, 1, axis=0))
        u2 = jnp.where(row == 0, prev2, jnp.where(row == 1, prev1, pltpu.roll(u, 2, axis=0)))
        tail = u[th - 2:th, :]
        conv = cw_ref[0:1, :] * u2 + cw_ref[1:2, :] * u1 + cw_ref[2:3, :] * u
        mixed.append((c_b * conv).astype(BF16))
    tail_scr[...] = tail

    o0 = 3 * CONV_W
    for rows, z in zip(subs, mixed):
        y_conv = jnp.dot(z, woc_ref[...], preferred_element_type=F32)
        y_attn = jnp.dot(y_ref[rows, :], woa_ref[...], preferred_element_type=F32)
        merged = (jax.nn.sigmoid(gate(rows, o0, o0 + D_MODEL)) * y_attn
                  + jax.nn.sigmoid(gate(rows, o0 + D_MODEL, o0 + 2 * D_MODEL)) * y_conv)
        act_scr[rows, :D_MODEL] = merged.astype(BF16)
    for rows in subs:
        x2 = x1_ref[rows, :] + jnp.dot(act_scr[rows, :D_MODEL], wout_ref[...],
                                       preferred_element_type=F32)
        o_ref[rows, :] = x2
        h_scr[rows, :] = _rms(x2, g3_ref[...]).astype(BF16)

    _swiglu_act(h_scr, act_scr, wg_ref, wu_ref, subs)
    for rows in subs:
        x3 = o_ref[rows, :] + FFN_RES * jnp.dot(act_scr[rows, :], wd_ref[...],
                                                preferred_element_type=F32)
        o_ref[rows, :] = _rms(x3, gf_ref[...])


def _split_w_in_body(w_ref, wa_ref, wg_ref):
    n_qkv = 3 * ATTN_W
    wa_ref[:, :n_qkv] = w_ref[:, :n_qkv].astype(BF16)
    tail = w_ref[:, n_qkv:n_qkv + LANES]
    lane = lax.broadcasted_iota(jnp.int32, tail.shape, 1)
    wa_ref[:, n_qkv:] = jnp.where(lane < N_HEADS, tail, 0.0).astype(BF16)
    wg_ref[...] = w_ref[:, n_qkv + N_HEADS:].astype(BF16)


def _resident(shape):
    return pl.BlockSpec(shape, lambda *_: (0,) * len(shape), pipeline_mode=pl.Buffered(1))


def kernel(x, ffn1_norm, ffn1_gate, ffn1_up, ffn1_down, mix_norm, w_in, b_forget, conv_w,
           w_o_attn, w_o_conv, w_out, ffn2_norm, ffn2_gate, ffn2_up, ffn2_down, final_norm):
    bsz, seq, d = x.shape
    assert d == D_MODEL and seq % TM_A == 0 and seq % TM_C == 0 and seq % TQ == 0
    row = lambda v: v.reshape(1, -1).astype(F32)
    w_a, w_gate = pl.pallas_call(
        _split_w_in_body,
        grid=(D_MODEL // W_IN_ROWS,),
        in_specs=[pl.BlockSpec((W_IN_ROWS, w_in.shape[1]), lambda i: (i, 0))],
        out_specs=[pl.BlockSpec((W_IN_ROWS, QKVF_COLS), lambda i: (i, 0)),
                   pl.BlockSpec((W_IN_ROWS, GATE_COLS), lambda i: (i, 0))],
        out_shape=[jax.ShapeDtypeStruct((D_MODEL, QKVF_COLS), BF16),
                   jax.ShapeDtypeStruct((D_MODEL, GATE_COLS), BF16)],
        compiler_params=pltpu.CompilerParams(dimension_semantics=("arbitrary",)),
        name="split_w_in",
    )(w_in)
    b_f = jnp.pad(row(b_forget), ((0, 0), (0, LANES - N_HEADS)))
    sel_q, sel_k = _bias_selectors()
    params = pltpu.CompilerParams(dimension_semantics=("arbitrary", "arbitrary"),
                                  vmem_limit_bytes=VMEM_LIMIT)

    tile_a = lambda w: pl.BlockSpec((None, TM_A, w), lambda b, i: (b, i, 0))
    x1, qk, vt, cum = pl.pallas_call(
        _ffn1_qkv_body,
        grid=(bsz, seq // TM_A),
        in_specs=[tile_a(D_MODEL), _resident((1, D_MODEL)),
                  _resident((D_MODEL, D_FF)), _resident((D_MODEL, D_FF)),
                  _resident((D_FF, D_MODEL)), _resident((1, D_MODEL)),
                  _resident((D_MODEL, QKVF_COLS)), _resident((1, LANES))],
        out_specs=[tile_a(D_MODEL), tile_a(2 * ATTN_W),
                   pl.BlockSpec((None, N_HEADS * VT_ROWS, TM_A), lambda b, i: (b, 0, i)),
                   tile_a(LANES)],
        out_shape=[jax.ShapeDtypeStruct((bsz, seq, D_MODEL), F32),
                   jax.ShapeDtypeStruct((bsz, seq, 2 * ATTN_W), BF16),
                   jax.ShapeDtypeStruct((bsz, N_HEADS * VT_ROWS, seq), BF16),
                   jax.ShapeDtypeStruct((bsz, seq, LANES), F32)],
        scratch_shapes=[pltpu.VMEM((TM_A, D_MODEL), BF16), pltpu.VMEM((TM_A, D_FF), BF16),
                        pltpu.VMEM((1, LANES), F32)],
        compiler_params=params,
        name="ffn1_qkv",
    )(x, row(ffn1_norm), ffn1_gate.astype(BF16), ffn1_up.astype(BF16),
      ffn1_down.astype(BF16), row(mix_norm), w_a, b_f)

    n_blk = seq // TQ
    y = pl.pallas_call(
        _fox_attn_body,
        grid=(bsz,),
        in_specs=[pl.BlockSpec((None, seq, 2 * ATTN_W), lambda b: (b, 0, 0)),
                  pl.BlockSpec((None, N_HEADS * VT_ROWS, seq), lambda b: (b, 0, 0)),
                  pl.BlockSpec((None, seq, LANES), lambda b: (b, 0, 0)),
                  _resident((N_SPLIT * LANES, ATTN_W)), _resident((N_SPLIT * LANES, ATTN_W))],
        out_specs=pl.BlockSpec((None, seq, ATTN_W), lambda b: (b, 0, 0)),
        out_shape=jax.ShapeDtypeStruct((bsz, seq, ATTN_W), BF16),
        scratch_shapes=[pltpu.VMEM((N_HEADS // 2, n_blk, 2 * TK, 2 * LANES), BF16),
                        pltpu.VMEM((n_blk, N_HEADS // 2, TQ, 2 * LANES), BF16),
                        pltpu.VMEM((2, N_HEADS, TQ), F32),
                        pltpu.VMEM((2, N_HEADS * VT_ROWS, TQ), F32),
                        pltpu.VMEM((2, ATTN_W, TQ), F32)],
        compiler_params=pltpu.CompilerParams(dimension_semantics=("arbitrary",),
                                             vmem_limit_bytes=VMEM_LIMIT),
        name="fox_attn",
    )(qk, vt, cum, sel_q, sel_k)

    tile_c = lambda w: pl.BlockSpec((None, TM_C, w), lambda b, i: (b, i, 0))
    return pl.pallas_call(
        _mix_ffn2_body,
        grid=(bsz, seq // TM_C),
        in_specs=[tile_c(D_MODEL), tile_c(ATTN_W), _resident((1, D_MODEL)),
                  _resident((D_MODEL, GATE_COLS)), _resident((3, CONV_W)),
                  _resident((ATTN_W, D_MODEL)), _resident((CONV_W, D_MODEL)),
                  _resident((D_MODEL, D_MODEL)), _resident((1, D_MODEL)),
                  _resident((D_MODEL, D_FF)), _resident((D_MODEL, D_FF)),
                  _resident((D_FF, D_MODEL)), _resident((1, D_MODEL))],
        out_specs=tile_c(D_MODEL),
        out_shape=jax.ShapeDtypeStruct((bsz, seq, D_MODEL), F32),
        scratch_shapes=[pltpu.VMEM((TM_C, D_MODEL), BF16), pltpu.VMEM((TM_C, D_FF), BF16),
                        pltpu.VMEM((2, CONV_W), F32)],
        compiler_params=params,
        name="mix_ffn2",
    )(x1, y, row(mix_norm), w_gate, conv_w.astype(F32), w_o_attn.astype(BF16),
      w_o_conv.astype(BF16), w_out.astype(BF16), row(ffn2_norm), ffn2_gate.astype(BF16),
      ffn2_up.astype(BF16), ffn2_down.astype(BF16), row(final_norm))
```

```python
import math

import numpy as np
import jax
import jax.numpy as jnp
from jax import lax
from jax.experimental import pallas as pl
from jax.experimental.pallas import tpu as pltpu

F32 = jnp.float32
BF16 = jnp.bfloat16

D_MODEL = 1024
N_HEADS = 8
HEAD_DIM = 64
ATTN_W = N_HEADS * HEAD_DIM
CONV_W = D_MODEL // 2
D_FF = 2816
RMS_EPS = 1e-6
FFN_RES = 0.5
LANES = 128
QKVF_COLS = 3 * ATTN_W + LANES
GATE_COLS = 3 * CONV_W + 2 * D_MODEL
FF_CHUNK = 256
TM_A = 1024
TM_C = 512
W_IN_ROWS = 256
SUB_A = 2
SUB_C = 2
TQ = 256
TK = TQ
SCORES_AHEAD = 3
LOG2E = 1.4426950408889634
NEG_BIG = -1e30
BF16_ROWS = 16
VT_ROWS = HEAD_DIM + BF16_ROWS
N_SPLIT = 3
BIAS_SLOT = 8
ONES_LANE = N_HEADS
VMEM_LIMIT = 56 * 1024 * 1024


def _rms(x, g):
    inv = lax.rsqrt(jnp.mean(x * x, axis=-1, keepdims=True) + RMS_EPS)
    return (x * inv) * g


def _sub_rows(tm, n_sub):
    th = tm // n_sub
    return [slice(i * th, (i + 1) * th) for i in range(n_sub)]


def _swiglu_act(h_scr, act_scr, wg_ref, wu_ref, subs, fill_h=None):
    for c in range(D_FF // FF_CHUNK):
        sl = slice(c * FF_CHUNK, (c + 1) * FF_CHUNK)
        for rows in subs:
            if c == 0 and fill_h is not None:
                fill_h(rows)
            g = jnp.dot(h_scr[rows, :], wg_ref[:, sl], preferred_element_type=F32)
            u = jnp.dot(h_scr[rows, :], wu_ref[:, sl], preferred_element_type=F32)
            act_scr[rows, sl] = (g * jax.nn.sigmoid(g) * u).astype(BF16)


def _cumsum_rows(x):
    n = x.shape[0]
    row = lax.broadcasted_iota(jnp.int32, x.shape, 0)
    d = 1
    while d < n:
        x = x + jnp.where(row >= d, pltpu.roll(x, d, axis=0), 0.0)
        d *= 2
    return x


def _ffn1_qkv_body(x_ref, g1_ref, wg_ref, wu_ref, wd_ref, g2_ref, wa_ref, bf_ref,
                   x1_ref, qk_ref, vt_ref, cc_ref,
                   h_scr, act_scr, carry_scr):
    subs = _sub_rows(TM_A, SUB_A)

    def fill_h(rows):
        h_scr[rows, :] = _rms(x_ref[rows, :], g1_ref[...]).astype(BF16)
    _swiglu_act(h_scr, act_scr, wg_ref, wu_ref, subs, fill_h)
    for rows in subs:
        x1 = x_ref[rows, :] + FFN_RES * jnp.dot(act_scr[rows, :], wd_ref[...],
                                                preferred_element_type=F32)
        x1_ref[rows, :] = x1
        h_scr[rows, :] = _rms(x1, g2_ref[...]).astype(BF16)

    @pl.when(pl.program_id(1) == 0)
    def _():
        carry_scr[...] = jnp.zeros_like(carry_scr)

    carry = carry_scr[...]
    for rows in subs:
        pr = jnp.dot(h_scr[rows, :], wa_ref[...], preferred_element_type=F32)
        qk_ref[rows, :ATTN_W] = (pr[:, :ATTN_W] * (LOG2E / math.sqrt(HEAD_DIM))).astype(BF16)
        qk_ref[rows, ATTN_W:] = pr[:, ATTN_W:2 * ATTN_W].astype(BF16)
        v_t = pr[:, 2 * ATTN_W:3 * ATTN_W].T.astype(BF16)
        for h in range(N_HEADS):
            vt_ref[h * VT_ROWS:h * VT_ROWS + HEAD_DIM, rows] = v_t[h * HEAD_DIM:(h + 1) * HEAD_DIM, :]
            vt_ref[h * VT_ROWS + HEAD_DIM:(h + 1) * VT_ROWS, rows] = jnp.ones(
                (VT_ROWS - HEAD_DIM, v_t.shape[1]), BF16)
        z = pr[:, 3 * ATTN_W:] + bf_ref[...]
        log_f = jnp.minimum(z, 0.0) - jnp.log1p(jnp.exp(-jnp.abs(z)))
        cum = _cumsum_rows(log_f) + carry
        carry = cum[cum.shape[0] - 1:, :]
        cc_ref[rows, :] = cum * LOG2E
    carry_scr[...] = carry


def _split_cum(c):
    pieces = []
    r = c
    for _ in range(N_SPLIT):
        p = r.astype(BF16)
        pieces.append(p)
        r = r - p.astype(F32)
    lane = lax.broadcasted_iota(jnp.int32, c.shape, 1)
    pieces[0] = jnp.where(lane == ONES_LANE, jnp.ones_like(pieces[0]), pieces[0])
    return jnp.concatenate(pieces, axis=1)


def _bias_selectors():
    sq = np.zeros((N_SPLIT * LANES, ATTN_W), np.float32)
    sk = np.zeros((N_SPLIT * LANES, ATTN_W), np.float32)
    for h in range(N_HEADS):
        base = (h // 2) * LANES + (h % 2) * BIAS_SLOT
        for s in range(N_SPLIT):
            sq[s * LANES + h, base + s] = 1.0
            sq[ONES_LANE, base + N_SPLIT + s] = 1.0
            sk[ONES_LANE, base + s] = 1.0
            sk[s * LANES + h, base + N_SPLIT + s] = -1.0
    return jnp.asarray(sq, BF16), jnp.asarray(sk, BF16)


def _fox_attn_body(qk_ref, vt_ref, cc_ref, sq_ref, sk_ref, y_ref,
                   kf_scr, qc_scr, m_scr, ot_scr, yt_scr):
    n_blk = qk_ref.shape[0] // TQ
    n_pairs = N_HEADS // 2
    lane = lax.broadcasted_iota(jnp.int32, (TK, LANES), 1)

    for r in range(n_blk):
        rows_r = slice(r * TK, (r + 1) * TK)
        pieces = _split_cum(cc_ref[rows_r, :])
        ka = jnp.dot(pieces, sk_ref[...], preferred_element_type=F32).astype(BF16)
        qa = jnp.dot(pieces, sq_ref[...], preferred_element_type=F32).astype(BF16)
        for pair in range(n_pairs):
            cols = slice(pair * LANES, (pair + 1) * LANES)
            kp = qk_ref[rows_r, ATTN_W + pair * LANES:ATTN_W + (pair + 1) * LANES]
            kap = ka[:, cols]
            for hh in range(2):
                own = (lane >= hh * HEAD_DIM) & (lane < (hh + 1) * HEAD_DIM)
                own_b = (lane >= hh * BIAS_SLOT) & (lane < (hh + 1) * BIAS_SLOT)
                rows = slice(hh * TK, (hh + 1) * TK)
                kf_scr[pair, r, rows, :LANES] = jnp.where(own, kp, jnp.zeros_like(kp))
                kf_scr[pair, r, rows, LANES:] = jnp.where(own_b, kap, jnp.zeros_like(kap))
            qc_scr[r, pair, :, :LANES] = qk_ref[rows_r, cols]
            qc_scr[r, pair, :, LANES:] = qa[:, cols]

    k_idx = lax.broadcasted_iota(jnp.int32, (TK, LANES), 0)
    q_idx = lax.broadcasted_iota(jnp.int32, (TK, LANES), 1)

    stream = [(qi, j, pair) for qi in range(n_blk) for j in range(qi + 1)
              for pair in range(n_pairs)]

    pending = {}

    def scores(n):
        qi, j, pair = stream[n]
        pending[n] = lax.dot_general(kf_scr[pair, j], qc_scr[qi, pair],
                                     (((1,), (1,)), ((), ())), preferred_element_type=F32)

    def head(st_pair, par, h, j, on_diagonal):
        alphas, ps = [], []
        for c in range(TQ // LANES):
            cols = slice(c * LANES, (c + 1) * LANES)
            st = st_pair[(h % 2) * TK:(h % 2 + 1) * TK, cols]
            if on_diagonal:
                st = jnp.where(q_idx + c * LANES >= k_idx, st, NEG_BIG)
            m_old = m_scr[par, h:h + 1, cols]
            m_new = jnp.maximum(m_old, jnp.max(st, axis=0, keepdims=True))
            alphas.append(jnp.exp2(m_old - m_new))
            ps.append(jnp.exp2(st - m_new).astype(BF16))
            m_scr[par, h:h + 1, cols] = m_new
        rows = slice(h * VT_ROWS, (h + 1) * VT_ROWS)
        pv = jnp.dot(vt_ref[rows, j * TK:(j + 1) * TK], jnp.concatenate(ps, axis=1),
                     preferred_element_type=F32)
        for c in range(TQ // LANES):
            cols = slice(c * LANES, (c + 1) * LANES)
            ot_scr[par, rows, cols] = alphas[c] * ot_scr[par, rows, cols] + pv[:, cols]

    for n in range(SCORES_AHEAD):
        scores(n)
    for n, (qi, j, pair) in enumerate(stream):
        par = qi % 2
        if n + SCORES_AHEAD < len(stream):
            scores(n + SCORES_AHEAD)
        if j == 0 and pair == 0:
            m_scr[par] = jnp.full(m_scr.shape[1:], NEG_BIG, F32)
            ot_scr[par] = jnp.zeros(ot_scr.shape[1:], F32)
        st_pair = pending.pop(n)
        head(st_pair, par, 2 * pair, j, j == qi)
        head(st_pair, par, 2 * pair + 1, j, j == qi)
        if j == qi and pair == n_pairs - 1:
            for h in range(N_HEADS):
                o = ot_scr[par, h * VT_ROWS:h * VT_ROWS + HEAD_DIM, :]
                l = ot_scr[par, h * VT_ROWS + HEAD_DIM:h * VT_ROWS + HEAD_DIM + 1, :]
                yt_scr[par, h * HEAD_DIM:(h + 1) * HEAD_DIM, :] = o / l
            y_ref[qi * TQ:(qi + 1) * TQ, :] = yt_scr[par].T.astype(BF16)


def _mix_ffn2_body(x1_ref, y_ref, gm_ref, wgate_ref, cw_ref, woa_ref, woc_ref, wout_ref,
                   g3_ref, wg_ref, wu_ref, wd_ref, gf_ref,
                   o_ref, h_scr, act_scr, tail_scr):
    subs = _sub_rows(TM_C, SUB_C)
    th = TM_C // SUB_C

    def gate(rows, lo, hi):
        return jnp.dot(h_scr[rows, :], wgate_ref[:, lo:hi], preferred_element_type=F32)

    @pl.when(pl.program_id(1) == 0)
    def _():
        tail_scr[...] = jnp.zeros_like(tail_scr)

    row = lax.broadcasted_iota(jnp.int32, (th, CONV_W), 0)
    tail = tail_scr[...]
    mixed = []
    for rows in subs:
        h_scr[rows, :] = _rms(x1_ref[rows, :], gm_ref[...]).astype(BF16)
        c_b = gate(rows, 0, CONV_W)
        u = gate(rows, CONV_W, 2 * CONV_W) * gate(rows, 2 * CONV_W, 3 * CONV_W)
        prev2, prev1 = tail[0:1, :], tail[1:2, :]
        u1 = jnp.where(row == 0, prev1, pltpu.roll(u, 1, axis=0))
        u2 = jnp.where(row == 0, prev2, jnp.where(row == 1, prev1, pltpu.roll(u, 2, axis=0)))
        tail = u[th - 2:th, :]
        conv = cw_ref[0:1, :] * u2 + cw_ref[1:2, :] * u1 + cw_ref[2:3, :] * u
        mixed.append((c_b * conv).astype(BF16))
    tail_scr[...] = tail

    o0 = 3 * CONV_W
    for rows, z in zip(subs, mixed):
        y_conv = jnp.dot(z, woc_ref[...], preferred_element_type=F32)
        y_attn = jnp.dot(y_ref[rows, :], woa_ref[...], preferred_element_type=F32)
        merged = (jax.nn.sigmoid(gate(rows, o0, o0 + D_MODEL)) * y_attn
                  + jax.nn.sigmoid(gate(rows, o0 + D_MODEL, o0 + 2 * D_MODEL)) * y_conv)
        act_scr[rows, :D_MODEL] = merged.astype(BF16)
    for rows in subs:
        x2 = x1_ref[rows, :] + jnp.dot(act_scr[rows, :D_MODEL], wout_ref[...],
                                       preferred_element_type=F32)
        o_ref[rows, :] = x2
        h_scr[rows, :] = _rms(x2, g3_ref[...]).astype(BF16)

    _swiglu_act(h_scr, act_scr, wg_ref, wu_ref, subs)
    for rows in subs:
        x3 = o_ref[rows, :] + FFN_RES * jnp.dot(act_scr[rows, :], wd_ref[...],
                                                preferred_element_type=F32)
        o_ref[rows, :] = _rms(x3, gf_ref[...])


def _split_w_in_body(w_ref, wa_ref, wg_ref):
    n_qkv = 3 * ATTN_W
    wa_ref[:, :n_qkv] = w_ref[:, :n_qkv].astype(BF16)
    tail = w_ref[:, n_qkv:n_qkv + LANES]
    lane = lax.broadcasted_iota(jnp.int32, tail.shape, 1)
    wa_ref[:, n_qkv:] = jnp.where(lane < N_HEADS, tail, 0.0).astype(BF16)
    wg_ref[...] = w_ref[:, n_qkv + N_HEADS:].astype(BF16)


def _resident(shape):
    return pl.BlockSpec(shape, lambda *_: (0,) * len(shape), pipeline_mode=pl.Buffered(1))


def kernel(x, ffn1_norm, ffn1_gate, ffn1_up, ffn1_down, mix_norm, w_in, b_forget, conv_w,
           w_o_attn, w_o_conv, w_out, ffn2_norm, ffn2_gate, ffn2_up, ffn2_down, final_norm):
    bsz, seq, d = x.shape
    assert d == D_MODEL and seq % TM_A == 0 and seq % TM_C == 0 and seq % TQ == 0
    row = lambda v: v.reshape(1, -1).astype(F32)
    w_a, w_gate = pl.pallas_call(
        _split_w_in_body,
        grid=(D_MODEL // W_IN_ROWS,),
        in_specs=[pl.BlockSpec((W_IN_ROWS, w_in.shape[1]), lambda i: (i, 0))],
        out_specs=[pl.BlockSpec((W_IN_ROWS, QKVF_COLS), lambda i: (i, 0)),
                   pl.BlockSpec((W_IN_ROWS, GATE_COLS), lambda i: (i, 0))],
        out_shape=[jax.ShapeDtypeStruct((D_MODEL, QKVF_COLS), BF16),
                   jax.ShapeDtypeStruct((D_MODEL, GATE_COLS), BF16)],
        compiler_params=pltpu.CompilerParams(dimension_semantics=("arbitrary",)),
        name="split_w_in",
    )(w_in)
    b_f = jnp.pad(row(b_forget), ((0, 0), (0, LANES - N_HEADS)))
    sel_q, sel_k = _bias_selectors()
    params = pltpu.CompilerParams(dimension_semantics=("arbitrary", "arbitrary"),
                                  vmem_limit_bytes=VMEM_LIMIT)

    tile_a = lambda w: pl.BlockSpec((None, TM_A, w), lambda b, i: (b, i, 0))
    x1, qk, vt, cum = pl.pallas_call(
        _ffn1_qkv_body,
        grid=(bsz, seq // TM_A),
        in_specs=[tile_a(D_MODEL), _resident((1, D_MODEL)),
                  _resident((D_MODEL, D_FF)), _resident((D_MODEL, D_FF)),
                  _resident((D_FF, D_MODEL)), _resident((1, D_MODEL)),
                  _resident((D_MODEL, QKVF_COLS)), _resident((1, LANES))],
        out_specs=[tile_a(D_MODEL), tile_a(2 * ATTN_W),
                   pl.BlockSpec((None, N_HEADS * VT_ROWS, TM_A), lambda b, i: (b, 0, i)),
                   tile_a(LANES)],
        out_shape=[jax.ShapeDtypeStruct((bsz, seq, D_MODEL), F32),
                   jax.ShapeDtypeStruct((bsz, seq, 2 * ATTN_W), BF16),
                   jax.ShapeDtypeStruct((bsz, N_HEADS * VT_ROWS, seq), BF16),
                   jax.ShapeDtypeStruct((bsz, seq, LANES), F32)],
        scratch_shapes=[pltpu.VMEM((TM_A, D_MODEL), BF16), pltpu.VMEM((TM_A, D_FF), BF16),
                        pltpu.VMEM((1, LANES), F32)],
        compiler_params=params,
        name="ffn1_qkv",
    )(x, row(ffn1_norm), ffn1_gate.astype(BF16), ffn1_up.astype(BF16),
      ffn1_down.astype(BF16), row(mix_norm), w_a, b_f)

    n_blk = seq // TQ
    y = pl.pallas_call(
        _fox_attn_body,
        grid=(bsz,),
        in_specs=[pl.BlockSpec((None, seq, 2 * ATTN_W), lambda b: (b, 0, 0)),
                  pl.BlockSpec((None, N_HEADS * VT_ROWS, seq), lambda b: (b, 0, 0)),
                  pl.BlockSpec((None, seq, LANES), lambda b: (b, 0, 0)),
                  _resident((N_SPLIT * LANES, ATTN_W)), _resident((N_SPLIT * LANES, ATTN_W))],
        out_specs=pl.BlockSpec((None, seq, ATTN_W), lambda b: (b, 0, 0)),
        out_shape=jax.ShapeDtypeStruct((bsz, seq, ATTN_W), BF16),
        scratch_shapes=[pltpu.VMEM((N_HEADS // 2, n_blk, 2 * TK, 2 * LANES), BF16),
                        pltpu.VMEM((n_blk, N_HEADS // 2, TQ, 2 * LANES), BF16),
                        pltpu.VMEM((2, N_HEADS, TQ), F32),
                        pltpu.VMEM((2, N_HEADS * VT_ROWS, TQ), F32),
                        pltpu.VMEM((2, ATTN_W, TQ), F32)],
        compiler_params=pltpu.CompilerParams(dimension_semantics=("arbitrary",),
                                             vmem_limit_bytes=VMEM_LIMIT),
        name="fox_attn",
    )(qk, vt, cum, sel_q, sel_k)

    tile_c = lambda w: pl.BlockSpec((None, TM_C, w), lambda b, i: (b, i, 0))
    return pl.pallas_call(
        _mix_ffn2_body,
        grid=(bsz, seq // TM_C),
        in_specs=[tile_c(D_MODEL), tile_c(ATTN_W), _resident((1, D_MODEL)),
                  _resident((D_MODEL, GATE_COLS)), _resident((3, CONV_W)),
                  _resident((ATTN_W, D_MODEL)), _resident((CONV_W, D_MODEL)),
                  _resident((D_MODEL, D_MODEL)), _resident((1, D_MODEL)),
                  _resident((D_MODEL, D_FF)), _resident((D_MODEL, D_FF)),
                  _resident((D_FF, D_MODEL)), _resident((1, D_MODEL))],
        out_specs=tile_c(D_MODEL),
        out_shape=jax.ShapeDtypeStruct((bsz, seq, D_MODEL), F32),
        scratch_shapes=[pltpu.VMEM((TM_C, D_MODEL), BF16), pltpu.VMEM((TM_C, D_FF), BF16),
                        pltpu.VMEM((2, CONV_W), F32)],
        compiler_params=params,
        name="mix_ffn2",
    )(x1, y, row(mix_norm), w_gate, conv_w.astype(F32), w_o_attn.astype(BF16),
      w_o_conv.astype(BF16), w_out.astype(BF16), row(ffn2_norm), ffn2_gate.astype(BF16),
      ffn2_up.astype(BF16), ffn2_down.astype(BF16), row(final_norm))
```

```python
import math

import numpy as np
import jax
import jax.numpy as jnp
from jax import lax
from jax.experimental import pallas as pl
from jax.experimental.pallas import tpu as pltpu

F32 = jnp.float32
BF16 = jnp.bfloat16

D_MODEL = 1024
N_HEADS = 8
HEAD_DIM = 64
ATTN_W = N_HEADS * HEAD_DIM
CONV_W = D_MODEL // 2
D_FF = 2816
RMS_EPS = 1e-6
FFN_RES = 0.5
LANES = 128
QKVF_COLS = 3 * ATTN_W + LANES
GATE_COLS = 3 * CONV_W + 2 * D_MODEL
FF_CHUNK = 256
TM_A = 1024
TM_C = 512
W_IN_ROWS = 256
SUB_A = 2
SUB_C = 2
TQ = 256
TK = TQ
SCORES_AHEAD = 3
LOG2E = 1.4426950408889634
NEG_BIG = -1e30
VT_ROWS = HEAD_DIM + 16
N_SPLIT = 3
BIAS_SLOT = 8
ONES_LANE = N_HEADS
VMEM_LIMIT = 56 * 1024 * 1024


def _rms(x, g):
    inv = lax.rsqrt(jnp.mean(x * x, axis=-1, keepdims=True) + RMS_EPS)
    return (x * inv) * g


def _sub_rows(tm, n_sub):
    th = tm // n_sub
    return [slice(i * th, (i + 1) * th) for i in range(n_sub)]


def _swiglu_act(h_scr, act_scr, wg_ref, wu_ref, subs, fill_h=None):
    for c in range(D_FF // FF_CHUNK):
        sl = slice(c * FF_CHUNK, (c + 1) * FF_CHUNK)
        for rows in subs:
            if c == 0 and fill_h is not None:
                fill_h(rows)
            g = jnp.dot(h_scr[rows, :], wg_ref[:, sl], preferred_element_type=F32)
            u = jnp.dot(h_scr[rows, :], wu_ref[:, sl], preferred_element_type=F32)
            act_scr[rows, sl] = (g * jax.nn.sigmoid(g) * u).astype(BF16)


def _cumsum_rows(x):
    n = x.shape[0]
    row = lax.broadcasted_iota(jnp.int32, x.shape, 0)
    d = 1
    while d < n:
        x = x + jnp.where(row >= d, pltpu.roll(x, d, axis=0), 0.0)
        d *= 2
    return x


def _ffn1_qkv_body(x_ref, g1_ref, wg_ref, wu_ref, wd_ref, g2_ref, wa_ref, bf_ref,
                   x1_ref, qk_ref, vt_ref, cc_ref,
                   h_scr, act_scr, carry_scr):
    subs = _sub_rows(TM_A, SUB_A)

    def fill_h(rows):
        h_scr[rows, :] = _rms(x_ref[rows, :], g1_ref[...]).astype(BF16)
    _swiglu_act(h_scr, act_scr, wg_ref, wu_ref, subs, fill_h)
    for rows in subs:
        x1 = x_ref[rows, :] + FFN_RES * jnp.dot(act_scr[rows, :], wd_ref[...],
                                                preferred_element_type=F32)
        x1_ref[rows, :] = x1
        h_scr[rows, :] = _rms(x1, g2_ref[...]).astype(BF16)

    @pl.when(pl.program_id(1) == 0)
    def _():
        carry_scr[...] = jnp.zeros_like(carry_scr)

    carry = carry_scr[...]
    for rows in subs:
        pr = jnp.dot(h_scr[rows, :], wa_ref[...], preferred_element_type=F32)
        qk_ref[rows, :ATTN_W] = (pr[:, :ATTN_W] * (LOG2E / math.sqrt(HEAD_DIM))).astype(BF16)
        qk_ref[rows, ATTN_W:] = pr[:, ATTN_W:2 * ATTN_W].astype(BF16)
        v_t = pr[:, 2 * ATTN_W:3 * ATTN_W].T.astype(BF16)
        for h in range(N_HEADS):
            vt_ref[h * VT_ROWS:h * VT_ROWS + HEAD_DIM, rows] = v_t[h * HEAD_DIM:(h + 1) * HEAD_DIM, :]
            vt_ref[h * VT_ROWS + HEAD_DIM:(h + 1) * VT_ROWS, rows] = jnp.ones(
                (VT_ROWS - HEAD_DIM, v_t.shape[1]), BF16)
        z = pr[:, 3 * ATTN_W:] + bf_ref[...]
        log_f = jnp.minimum(z, 0.0) - jnp.log1p(jnp.exp(-jnp.abs(z)))
        cum = _cumsum_rows(log_f) + carry
        carry = cum[cum.shape[0] - 1:, :]
        cc_ref[rows, :] = cum * LOG2E
    carry_scr[...] = carry


def _split_cum(c):
    pieces = []
    r = c
    for _ in range(N_SPLIT):
        p = r.astype(BF16)
        pieces.append(p)
        r = r - p.astype(F32)
    lane = lax.broadcasted_iota(jnp.int32, c.shape, 1)
    pieces[0] = jnp.where(lane == ONES_LANE, jnp.ones_like(pieces[0]), pieces[0])
    return jnp.concatenate(pieces, axis=1)


def _bias_selectors():
    sq = np.zeros((N_SPLIT * LANES, ATTN_W), np.float32)
    sk = np.zeros((N_SPLIT * LANES, ATTN_W), np.float32)
    for h in range(N_HEADS):
        base = (h // 2) * LANES + (h % 2) * BIAS_SLOT
        for s in range(N_SPLIT):
            sq[s * LANES + h, base + s] = 1.0
            sq[ONES_LANE, base + N_SPLIT + s] = 1.0
            sk[ONES_LANE, base + s] = 1.0
            sk[s * LANES + h, base + N_SPLIT + s] = -1.0
    return jnp.asarray(sq, BF16), jnp.asarray(sk, BF16)


def _fox_attn_body(qk_ref, vt_ref, cc_ref, sq_ref, sk_ref, woa_ref, y_ref,
                   kf_scr, qc_scr, m_scr, ot_scr, yt_scr):
    n_blk = qk_ref.shape[0] // TQ
    n_pairs = N_HEADS // 2
    lane = lax.broadcasted_iota(jnp.int32, (TK, LANES), 1)

    for r in range(n_blk):
        rows_r = slice(r * TK, (r + 1) * TK)
        pieces = _split_cum(cc_ref[rows_r, :])
        ka = jnp.dot(pieces, sk_ref[...], preferred_element_type=F32).astype(BF16)
        qa = jnp.dot(pieces, sq_ref[...], preferred_element_type=F32).astype(BF16)
        for pair in range(n_pairs):
            cols = slice(pair * LANES, (pair + 1) * LANES)
            kp = qk_ref[rows_r, ATTN_W + pair * LANES:ATTN_W + (pair + 1) * LANES]
            kap = ka[:, cols]
            for hh in range(2):
                own = (lane >= hh * HEAD_DIM) & (lane < (hh + 1) * HEAD_DIM)
                own_b = (lane >= hh * BIAS_SLOT) & (lane < (hh + 1) * BIAS_SLOT)
                rows = slice(hh * TK, (hh + 1) * TK)
                kf_scr[pair, r, rows, :LANES] = jnp.where(own, kp, jnp.zeros_like(kp))
                kf_scr[pair, r, rows, LANES:] = jnp.where(own_b, kap, jnp.zeros_like(kap))
            qc_scr[r, pair, :, :LANES] = qk_ref[rows_r, cols]
            qc_scr[r, pair, :, LANES:] = qa[:, cols]

    k_idx = lax.broadcasted_iota(jnp.int32, (TK, LANES), 0)
    q_idx = lax.broadcasted_iota(jnp.int32, (TK, LANES), 1)

    stream = [(qi, j, pair) for qi in range(n_blk) for j in range(qi + 1)
              for pair in range(n_pairs)]

    pending = {}

    def scores(n):
        qi, j, pair = stream[n]
        pending[n] = lax.dot_general(kf_scr[pair, j], qc_scr[qi, pair],
                                     (((1,), (1,)), ((), ())), preferred_element_type=F32)

    def head(st_pair, par, h, j, on_diagonal):
        alphas, ps = [], []
        for c in range(TQ // LANES):
            cols = slice(c * LANES, (c + 1) * LANES)
            st = st_pair[(h % 2) * TK:(h % 2 + 1) * TK, cols]
            if on_diagonal:
                st = jnp.where(q_idx + c * LANES >= k_idx, st, NEG_BIG)
            m_old = m_scr[par, h:h + 1, cols]
            m_new = jnp.maximum(m_old, jnp.max(st, axis=0, keepdims=True))
            alphas.append(jnp.exp2(m_old - m_new))
            ps.append(jnp.exp2(st - m_new).astype(BF16))
            m_scr[par, h:h + 1, cols] = m_new
        rows = slice(h * VT_ROWS, (h + 1) * VT_ROWS)
        pv = jnp.dot(vt_ref[rows, j * TK:(j + 1) * TK], jnp.concatenate(ps, axis=1),
                     preferred_element_type=F32)
        for c in range(TQ // LANES):
            cols = slice(c * LANES, (c + 1) * LANES)
            ot_scr[par, rows, cols] = alphas[c] * ot_scr[par, rows, cols] + pv[:, cols]

    for n in range(SCORES_AHEAD):
        scores(n)
    for n, (qi, j, pair) in enumerate(stream):
        par = qi % 2
        if n + SCORES_AHEAD < len(stream):
            scores(n + SCORES_AHEAD)
        if j == 0 and pair == 0:
            m_scr[par] = jnp.full(m_scr.shape[1:], NEG_BIG, F32)
            ot_scr[par] = jnp.zeros(ot_scr.shape[1:], F32)
        st_pair = pending.pop(n)
        head(st_pair, par, 2 * pair, j, j == qi)
        head(st_pair, par, 2 * pair + 1, j, j == qi)
        if j == qi and pair == n_pairs - 1:
            for h in range(N_HEADS):
                o = ot_scr[par, h * VT_ROWS:h * VT_ROWS + HEAD_DIM, :]
                l = ot_scr[par, h * VT_ROWS + HEAD_DIM:h * VT_ROWS + HEAD_DIM + 1, :]
                yt_scr[par, h * HEAD_DIM:(h + 1) * HEAD_DIM, :] = o / l
            y_ref[qi * TQ:(qi + 1) * TQ, :] = jnp.dot(
                yt_scr[par].T.astype(BF16), woa_ref[...], preferred_element_type=F32)


def _mix_ffn2_body(x1_ref, y_ref, gm_ref, wgate_ref, cw_ref, woc_ref, wout_ref,
                   g3_ref, wg_ref, wu_ref, wd_ref, gf_ref,
                   o_ref, h_scr, act_scr, tail_scr):
    subs = _sub_rows(TM_C, SUB_C)
    th = TM_C // SUB_C

    def gate(rows, lo, hi):
        return jnp.dot(h_scr[rows, :], wgate_ref[:, lo:hi], preferred_element_type=F32)

    @pl.when(pl.program_id(1) == 0)
    def _():
        tail_scr[...] = jnp.zeros_like(tail_scr)

    row = lax.broadcasted_iota(jnp.int32, (th, CONV_W), 0)
    tail = tail_scr[...]
    mixed = []
    for rows in subs:
        h_scr[rows, :] = _rms(x1_ref[rows, :], gm_ref[...]).astype(BF16)
        c_b = gate(rows, 0, CONV_W)
        u = gate(rows, CONV_W, 2 * CONV_W) * gate(rows, 2 * CONV_W, 3 * CONV_W)
        prev2, prev1 = tail[0:1, :], tail[1:2, :]
        u1 = jnp.where(row == 0, prev1, pltpu.roll(u, 1, axis=0))
        u2 = jnp.where(row == 0, prev2, jnp.where(row == 1, prev1, pltpu.roll(u, 2, axis=0)))
        tail = u[th - 2:th, :]
        conv = cw_ref[0:1, :] * u2 + cw_ref[1:2, :] * u1 + cw_ref[2:3, :] * u
        mixed.append((c_b * conv).astype(BF16))
    tail_scr[...] = tail

    o0 = 3 * CONV_W
    for rows, z in zip(subs, mixed):
        y_conv = jnp.dot(z, woc_ref[...], preferred_element_type=F32)
        y_attn = y_ref[rows, :]
        merged = (jax.nn.sigmoid(gate(rows, o0, o0 + D_MODEL)) * y_attn
                  + jax.nn.sigmoid(gate(rows, o0 + D_MODEL, o0 + 2 * D_MODEL)) * y_conv)
        act_scr[rows, :D_MODEL] = merged.astype(BF16)
    for rows in subs:
        x2 = x1_ref[rows, :] + jnp.dot(act_scr[rows, :D_MODEL], wout_ref[...],
                                       preferred_element_type=F32)
        o_ref[rows, :] = x2
        h_scr[rows, :] = _rms(x2, g3_ref[...]).astype(BF16)

    _swiglu_act(h_scr, act_scr, wg_ref, wu_ref, subs)
    for rows in subs:
        x3 = o_ref[rows, :] + FFN_RES * jnp.dot(act_scr[rows, :], wd_ref[...],
                                                preferred_element_type=F32)
        o_ref[rows, :] = _rms(x3, gf_ref[...])


def _split_w_in_body(w_ref, wa_ref, wg_ref):
    n_qkv = 3 * ATTN_W
    wa_ref[:, :n_qkv] = w_ref[:, :n_qkv].astype(BF16)
    tail = w_ref[:, n_qkv:n_qkv + LANES]
    lane = lax.broadcasted_iota(jnp.int32, tail.shape, 1)
    wa_ref[:, n_qkv:] = jnp.where(lane < N_HEADS, tail, 0.0).astype(BF16)
    wg_ref[...] = w_ref[:, n_qkv + N_HEADS:].astype(BF16)


def _resident(shape):
    return pl.BlockSpec(shape, lambda *_: (0,) * len(shape), pipeline_mode=pl.Buffered(1))


def kernel(x, ffn1_norm, ffn1_gate, ffn1_up, ffn1_down, mix_norm, w_in, b_forget, conv_w,
           w_o_attn, w_o_conv, w_out, ffn2_norm, ffn2_gate, ffn2_up, ffn2_down, final_norm):
    bsz, seq, d = x.shape
    assert d == D_MODEL and seq % TM_A == 0 and seq % TM_C == 0 and seq % TQ == 0
    row = lambda v: v.reshape(1, -1).astype(F32)
    w_a, w_gate = pl.pallas_call(
        _split_w_in_body,
        grid=(D_MODEL // W_IN_ROWS,),
        in_specs=[pl.BlockSpec((W_IN_ROWS, w_in.shape[1]), lambda i: (i, 0))],
        out_specs=[pl.BlockSpec((W_IN_ROWS, QKVF_COLS), lambda i: (i, 0)),
                   pl.BlockSpec((W_IN_ROWS, GATE_COLS), lambda i: (i, 0))],
        out_shape=[jax.ShapeDtypeStruct((D_MODEL, QKVF_COLS), BF16),
                   jax.ShapeDtypeStruct((D_MODEL, GATE_COLS), BF16)],
        compiler_params=pltpu.CompilerParams(dimension_semantics=("arbitrary",)),
        name="split_w_in",
    )(w_in)
    b_f = jnp.pad(row(b_forget), ((0, 0), (0, LANES - N_HEADS)))
    sel_q, sel_k = _bias_selectors()
    params = pltpu.CompilerParams(dimension_semantics=("arbitrary", "arbitrary"),
                                  vmem_limit_bytes=VMEM_LIMIT)

    tile_a = lambda w: pl.BlockSpec((None, TM_A, w), lambda b, i: (b, i, 0))
    x1, qk, vt, cum = pl.pallas_call(
        _ffn1_qkv_body,
        grid=(bsz, seq // TM_A),
        in_specs=[tile_a(D_MODEL), _resident((1, D_MODEL)),
                  _resident((D_MODEL, D_FF)), _resident((D_MODEL, D_FF)),
                  _resident((D_FF, D_MODEL)), _resident((1, D_MODEL)),
                  _resident((D_MODEL, QKVF_COLS)), _resident((1, LANES))],
        out_specs=[tile_a(D_MODEL), tile_a(2 * ATTN_W),
                   pl.BlockSpec((None, N_HEADS * VT_ROWS, TM_A), lambda b, i: (b, 0, i)),
                   tile_a(LANES)],
        out_shape=[jax.ShapeDtypeStruct((bsz, seq, D_MODEL), F32),
                   jax.ShapeDtypeStruct((bsz, seq, 2 * ATTN_W), BF16),
                   jax.ShapeDtypeStruct((bsz, N_HEADS * VT_ROWS, seq), BF16),
                   jax.ShapeDtypeStruct((bsz, seq, LANES), F32)],
        scratch_shapes=[pltpu.VMEM((TM_A, D_MODEL), BF16), pltpu.VMEM((TM_A, D_FF), BF16),
                        pltpu.VMEM((1, LANES), F32)],
        compiler_params=params,
        name="ffn1_qkv",
    )(x, row(ffn1_norm), ffn1_gate.astype(BF16), ffn1_up.astype(BF16),
      ffn1_down.astype(BF16), row(mix_norm), w_a, b_f)

    n_blk = seq // TQ
    y = pl.pallas_call(
        _fox_attn_body,
        grid=(bsz,),
        in_specs=[pl.BlockSpec((None, seq, 2 * ATTN_W), lambda b: (b, 0, 0)),
                  pl.BlockSpec((None, N_HEADS * VT_ROWS, seq), lambda b: (b, 0, 0)),
                  pl.BlockSpec((None, seq, LANES), lambda b: (b, 0, 0)),
                  _resident((N_SPLIT * LANES, ATTN_W)), _resident((N_SPLIT * LANES, ATTN_W)),
                  _resident((ATTN_W, D_MODEL))],
        out_specs=pl.BlockSpec((None, seq, D_MODEL), lambda b: (b, 0, 0)),
        out_shape=jax.ShapeDtypeStruct((bsz, seq, D_MODEL), F32),
        scratch_shapes=[pltpu.VMEM((N_HEADS // 2, n_blk, 2 * TK, 2 * LANES), BF16),
                        pltpu.VMEM((n_blk, N_HEADS // 2, TQ, 2 * LANES), BF16),
                        pltpu.VMEM((2, N_HEADS, TQ), F32),
                        pltpu.VMEM((2, N_HEADS * VT_ROWS, TQ), F32),
                        pltpu.VMEM((2, ATTN_W, TQ), F32)],
        compiler_params=pltpu.CompilerParams(dimension_semantics=("arbitrary",),
                                             vmem_limit_bytes=VMEM_LIMIT),
        name="fox_attn",
    )(qk, vt, cum, sel_q, sel_k, w_o_attn.astype(BF16))

    tile_c = lambda w: pl.BlockSpec((None, TM_C, w), lambda b, i: (b, i, 0))
    return pl.pallas_call(
        _mix_ffn2_body,
        grid=(bsz, seq // TM_C),
        in_specs=[tile_c(D_MODEL), tile_c(D_MODEL), _resident((1, D_MODEL)),
                  _resident((D_MODEL, GATE_COLS)), _resident((3, CONV_W)),
                  _resident((CONV_W, D_MODEL)),
                  _resident((D_MODEL, D_MODEL)), _resident((1, D_MODEL)),
                  _resident((D_MODEL, D_FF)), _resident((D_MODEL, D_FF)),
                  _resident((D_FF, D_MODEL)), _resident((1, D_MODEL))],
        out_specs=tile_c(D_MODEL),
        out_shape=jax.ShapeDtypeStruct((bsz, seq, D_MODEL), F32),
        scratch_shapes=[pltpu.VMEM((TM_C, D_MODEL), BF16), pltpu.VMEM((TM_C, D_FF), BF16),
                        pltpu.VMEM((2, CONV_W), F32)],
        compiler_params=params,
        name="mix_ffn2",
    )(x1, y, row(mix_norm), w_gate, conv_w.astype(F32),
      w_o_conv.astype(BF16), w_out.astype(BF16), row(ffn2_norm), ffn2_gate.astype(BF16),
      ffn2_up.astype(BF16), ffn2_down.astype(BF16), row(final_norm))
```

```python
import math

import numpy as np
import jax
import jax.numpy as jnp
from jax import lax
from jax.experimental import pallas as pl
from jax.experimental.pallas import tpu as pltpu

F32 = jnp.float32
BF16 = jnp.bfloat16

D_MODEL = 1024
N_HEADS = 8
HEAD_DIM = 64
ATTN_W = N_HEADS * HEAD_DIM
CONV_W = D_MODEL // 2
D_FF = 2816
RMS_EPS = 1e-6
FFN_RES = 0.5
LANES = 128
QKVF_COLS = 3 * ATTN_W + LANES
GATE_COLS = 3 * CONV_W + 2 * D_MODEL
FF_CHUNK = 1408
TM_A = 1024
TM_C = 1024
W_IN_ROWS = 256
SUB_A = 2
SUB_C = 4
TQ = 256
TK = TQ
SCORES_AHEAD = 3
LOG2E = 1.4426950408889634
NEG_BIG = -1e30
VT_ROWS = HEAD_DIM + 16
N_SPLIT = 3
BIAS_SLOT = 8
ONES_LANE = N_HEADS
VMEM_LIMIT = 56 * 1024 * 1024


def _rms(x, g):
    inv = lax.rsqrt(jnp.mean(x * x, axis=-1, keepdims=True) + RMS_EPS)
    return (x * inv) * g


def _sub_rows(tm, n_sub):
    th = tm // n_sub
    return [slice(i * th, (i + 1) * th) for i in range(n_sub)]


def _swiglu_act(h_scr, act_scr, wg_ref, wu_ref, subs, fill_h=None):
    for c in range(D_FF // FF_CHUNK):
        sl = slice(c * FF_CHUNK, (c + 1) * FF_CHUNK)
        for rows in subs:
            if c == 0 and fill_h is not None:
                fill_h(rows)
            g = jnp.dot(h_scr[rows, :], wg_ref[:, sl], preferred_element_type=F32)
            u = jnp.dot(h_scr[rows, :], wu_ref[:, sl], preferred_element_type=F32)
            act_scr[rows, sl] = (g * jax.nn.sigmoid(g) * u).astype(BF16)


def _cumsum_rows(x):
    n = x.shape[0]
    row = lax.broadcasted_iota(jnp.int32, x.shape, 0)
    d = 1
    while d < n:
        x = x + jnp.where(row >= d, pltpu.roll(x, d, axis=0), 0.0)
        d *= 2
    return x


def _ffn1_qkv_body(x_ref, g1_ref, wg_ref, wu_ref, wd_ref, g2_ref, wa_ref, bf_ref,
                   x1_ref, qk_ref, vt_ref, cc_ref,
                   h_scr, act_scr, carry_scr):
    subs = _sub_rows(TM_A, SUB_A)

    def fill_h(rows):
        h_scr[rows, :] = _rms(x_ref[rows, :], g1_ref[...]).astype(BF16)
    _swiglu_act(h_scr, act_scr, wg_ref, wu_ref, subs, fill_h)
    for rows in subs:
        x1 = x_ref[rows, :] + FFN_RES * jnp.dot(act_scr[rows, :], wd_ref[...],
                                                preferred_element_type=F32)
        x1_ref[rows, :] = x1
        h_scr[rows, :] = _rms(x1, g2_ref[...]).astype(BF16)

    @pl.when(pl.program_id(1) == 0)
    def _():
        carry_scr[...] = jnp.zeros_like(carry_scr)

    carry = carry_scr[...]
    for rows in subs:
        pr = jnp.dot(h_scr[rows, :], wa_ref[...], preferred_element_type=F32)
        qk_ref[rows, :ATTN_W] = (pr[:, :ATTN_W] * (LOG2E / math.sqrt(HEAD_DIM))).astype(BF16)
        qk_ref[rows, ATTN_W:] = pr[:, ATTN_W:2 * ATTN_W].astype(BF16)
        v_t = pr[:, 2 * ATTN_W:3 * ATTN_W].T.astype(BF16)
        for h in range(N_HEADS):
            vt_ref[h * VT_ROWS:h * VT_ROWS + HEAD_DIM, rows] = v_t[h * HEAD_DIM:(h + 1) * HEAD_DIM, :]
            vt_ref[h * VT_ROWS + HEAD_DIM:(h + 1) * VT_ROWS, rows] = jnp.ones(
                (VT_ROWS - HEAD_DIM, v_t.shape[1]), BF16)
        z = pr[:, 3 * ATTN_W:] + bf_ref[...]
        log_f = jnp.minimum(z, 0.0) - jnp.log1p(jnp.exp(-jnp.abs(z)))
        cum = _cumsum_rows(log_f) + carry
        carry = cum[cum.shape[0] - 1:, :]
        cc_ref[rows, :] = cum * LOG2E
    carry_scr[...] = carry


def _split_cum(c):
    pieces = []
    r = c
    for _ in range(N_SPLIT):
        p = r.astype(BF16)
        pieces.append(p)
        r = r - p.astype(F32)
    lane = lax.broadcasted_iota(jnp.int32, c.shape, 1)
    pieces[0] = jnp.where(lane == ONES_LANE, jnp.ones_like(pieces[0]), pieces[0])
    return jnp.concatenate(pieces, axis=1)


def _bias_selectors():
    sq = np.zeros((N_SPLIT * LANES, ATTN_W), np.float32)
    sk = np.zeros((N_SPLIT * LANES, ATTN_W), np.float32)
    for h in range(N_HEADS):
        base = (h // 2) * LANES + (h % 2) * BIAS_SLOT
        for s in range(N_SPLIT):
            sq[s * LANES + h, base + s] = 1.0
            sq[ONES_LANE, base + N_SPLIT + s] = 1.0
            sk[ONES_LANE, base + s] = 1.0
            sk[s * LANES + h, base + N_SPLIT + s] = -1.0
    return jnp.asarray(sq, BF16), jnp.asarray(sk, BF16)


def _fox_attn_body(qk_ref, vt_ref, cc_ref, sq_ref, sk_ref, y_ref,
                   kf_scr, qc_scr, m_scr, ot_scr, yt_scr):
    n_blk = qk_ref.shape[0] // TQ
    n_pairs = N_HEADS // 2
    lane = lax.broadcasted_iota(jnp.int32, (TK, LANES), 1)

    for r in range(n_blk):
        rows_r = slice(r * TK, (r + 1) * TK)
        pieces = _split_cum(cc_ref[rows_r, :])
        ka = jnp.dot(pieces, sk_ref[...], preferred_element_type=F32).astype(BF16)
        qa = jnp.dot(pieces, sq_ref[...], preferred_element_type=F32).astype(BF16)
        for pair in range(n_pairs):
            cols = slice(pair * LANES, (pair + 1) * LANES)
            kp = qk_ref[rows_r, ATTN_W + pair * LANES:ATTN_W + (pair + 1) * LANES]
            kap = ka[:, cols]
            for hh in range(2):
                own = (lane >= hh * HEAD_DIM) & (lane < (hh + 1) * HEAD_DIM)
                own_b = (lane >= hh * BIAS_SLOT) & (lane < (hh + 1) * BIAS_SLOT)
                rows = slice(hh * TK, (hh + 1) * TK)
                kf_scr[pair, r, rows, :LANES] = jnp.where(own, kp, jnp.zeros_like(kp))
                kf_scr[pair, r, rows, LANES:] = jnp.where(own_b, kap, jnp.zeros_like(kap))
            qc_scr[r, pair, :, :LANES] = qk_ref[rows_r, cols]
            qc_scr[r, pair, :, LANES:] = qa[:, cols]

    k_idx = lax.broadcasted_iota(jnp.int32, (TK, LANES), 0)
    q_idx = lax.broadcasted_iota(jnp.int32, (TK, LANES), 1)

    stream = [(qi, j, pair) for qi in range(n_blk) for j in range(qi + 1)
              for pair in range(n_pairs)]

    pending = {}

    def scores(n):
        qi, j, pair = stream[n]
        pending[n] = lax.dot_general(kf_scr[pair, j], qc_scr[qi, pair],
                                     (((1,), (1,)), ((), ())), preferred_element_type=F32)

    def head(st_pair, par, h, j, on_diagonal):
        alphas, ps = [], []
        for c in range(TQ // LANES):
            cols = slice(c * LANES, (c + 1) * LANES)
            st = st_pair[(h % 2) * TK:(h % 2 + 1) * TK, cols]
            if on_diagonal:
                st = jnp.where(q_idx + c * LANES >= k_idx, st, NEG_BIG)
            m_old = m_scr[par, h:h + 1, cols]
            m_new = jnp.maximum(m_old, jnp.max(st, axis=0, keepdims=True))
            alphas.append(jnp.exp2(m_old - m_new))
            ps.append(jnp.exp2(st - m_new).astype(BF16))
            m_scr[par, h:h + 1, cols] = m_new
        rows = slice(h * VT_ROWS, (h + 1) * VT_ROWS)
        pv = jnp.dot(vt_ref[rows, j * TK:(j + 1) * TK], jnp.concatenate(ps, axis=1),
                     preferred_element_type=F32)
        for c in range(TQ // LANES):
            cols = slice(c * LANES, (c + 1) * LANES)
            ot_scr[par, rows, cols] = alphas[c] * ot_scr[par, rows, cols] + pv[:, cols]

    for n in range(SCORES_AHEAD):
        scores(n)
    for n, (qi, j, pair) in enumerate(stream):
        par = qi % 2
        if n + SCORES_AHEAD < len(stream):
            scores(n + SCORES_AHEAD)
        if j == 0 and pair == 0:
            m_scr[par] = jnp.full(m_scr.shape[1:], NEG_BIG, F32)
            ot_scr[par] = jnp.zeros(ot_scr.shape[1:], F32)
        st_pair = pending.pop(n)
        head(st_pair, par, 2 * pair, j, j == qi)
        head(st_pair, par, 2 * pair + 1, j, j == qi)
        if j == qi and pair == n_pairs - 1:
            for h in range(N_HEADS):
                o = ot_scr[par, h * VT_ROWS:h * VT_ROWS + HEAD_DIM, :]
                l = ot_scr[par, h * VT_ROWS + HEAD_DIM:h * VT_ROWS + HEAD_DIM + 1, :]
                yt_scr[par, h * HEAD_DIM:(h + 1) * HEAD_DIM, :] = o / l
            y_ref[qi * TQ:(qi + 1) * TQ, :] = yt_scr[par].T.astype(BF16)


def _mix_ffn2_body(x1_ref, y_ref, gm_ref, wgate_ref, cw_ref, woa_ref, woc_ref, wout_ref,
                   g3_ref, wg_ref, wu_ref, wd_ref, gf_ref,
                   o_ref, h_scr, act_scr, tail_scr):
    subs = _sub_rows(TM_C, SUB_C)
    th = TM_C // SUB_C

    def gate(rows, lo, hi):
        return jnp.dot(h_scr[rows, :], wgate_ref[:, lo:hi], preferred_element_type=F32)

    @pl.when(pl.program_id(1) == 0)
    def _():
        tail_scr[...] = jnp.zeros_like(tail_scr)

    row = lax.broadcasted_iota(jnp.int32, (th, CONV_W), 0)
    tail = tail_scr[...]
    mixed = []
    for rows in subs:
        h_scr[rows, :] = _rms(x1_ref[rows, :], gm_ref[...]).astype(BF16)
        c_b = gate(rows, 0, CONV_W)
        u = gate(rows, CONV_W, 2 * CONV_W) * gate(rows, 2 * CONV_W, 3 * CONV_W)
        prev2, prev1 = tail[0:1, :], tail[1:2, :]
        u1 = jnp.where(row == 0, prev1, pltpu.roll(u, 1, axis=0))
        u2 = jnp.where(row == 0, prev2, jnp.where(row == 1, prev1, pltpu.roll(u, 2, axis=0)))
        tail = u[th - 2:th, :]
        conv = cw_ref[0:1, :] * u2 + cw_ref[1:2, :] * u1 + cw_ref[2:3, :] * u
        mixed.append((c_b * conv).astype(BF16))
    tail_scr[...] = tail

    o0 = 3 * CONV_W
    for rows, z in zip(subs, mixed):
        y_conv = jnp.dot(z, woc_ref[...], preferred_element_type=F32)
        y_attn = jnp.dot(y_ref[rows, :], woa_ref[...], preferred_element_type=F32)
        merged = (jax.nn.sigmoid(gate(rows, o0, o0 + D_MODEL)) * y_attn
                  + jax.nn.sigmoid(gate(rows, o0 + D_MODEL, o0 + 2 * D_MODEL)) * y_conv)
        act_scr[rows, :D_MODEL] = merged.astype(BF16)
    for rows in subs:
        x2 = x1_ref[rows, :] + jnp.dot(act_scr[rows, :D_MODEL], wout_ref[...],
                                       preferred_element_type=F32)
        o_ref[rows, :] = x2
        h_scr[rows, :] = _rms(x2, g3_ref[...]).astype(BF16)

    _swiglu_act(h_scr, act_scr, wg_ref, wu_ref, subs)
    for rows in subs:
        x3 = o_ref[rows, :] + FFN_RES * jnp.dot(act_scr[rows, :], wd_ref[...],
                                                preferred_element_type=F32)
        o_ref[rows, :] = _rms(x3, gf_ref[...])


def _split_w_in_body(w_ref, wa_ref, wg_ref):
    n_qkv = 3 * ATTN_W
    wa_ref[:, :n_qkv] = w_ref[:, :n_qkv].astype(BF16)
    tail = w_ref[:, n_qkv:n_qkv + LANES]
    lane = lax.broadcasted_iota(jnp.int32, tail.shape, 1)
    wa_ref[:, n_qkv:] = jnp.where(lane < N_HEADS, tail, 0.0).astype(BF16)
    wg_ref[...] = w_ref[:, n_qkv + N_HEADS:].astype(BF16)


def _resident(shape):
    return pl.BlockSpec(shape, lambda *_: (0,) * len(shape), pipeline_mode=pl.Buffered(1))


def kernel(x, ffn1_norm, ffn1_gate, ffn1_up, ffn1_down, mix_norm, w_in, b_forget, conv_w,
           w_o_attn, w_o_conv, w_out, ffn2_norm, ffn2_gate, ffn2_up, ffn2_down, final_norm):
    bsz, seq, d = x.shape
    assert d == D_MODEL and seq % TM_A == 0 and seq % TM_C == 0 and seq % TQ == 0
    row = lambda v: v.reshape(1, -1).astype(F32)
    w_a, w_gate = pl.pallas_call(
        _split_w_in_body,
        grid=(D_MODEL // W_IN_ROWS,),
        in_specs=[pl.BlockSpec((W_IN_ROWS, w_in.shape[1]), lambda i: (i, 0))],
        out_specs=[pl.BlockSpec((W_IN_ROWS, QKVF_COLS), lambda i: (i, 0)),
                   pl.BlockSpec((W_IN_ROWS, GATE_COLS), lambda i: (i, 0))],
        out_shape=[jax.ShapeDtypeStruct((D_MODEL, QKVF_COLS), BF16),
                   jax.ShapeDtypeStruct((D_MODEL, GATE_COLS), BF16)],
        compiler_params=pltpu.CompilerParams(dimension_semantics=("arbitrary",)),
        name="split_w_in",
    )(w_in)
    b_f = jnp.pad(row(b_forget), ((0, 0), (0, LANES - N_HEADS)))
    sel_q, sel_k = _bias_selectors()
    params = pltpu.CompilerParams(dimension_semantics=("arbitrary", "arbitrary"),
                                  vmem_limit_bytes=VMEM_LIMIT)

    tile_a = lambda w: pl.BlockSpec((None, TM_A, w), lambda b, i: (b, i, 0))
    x1, qk, vt, cum = pl.pallas_call(
        _ffn1_qkv_body,
        grid=(bsz, seq // TM_A),
        in_specs=[tile_a(D_MODEL), _resident((1, D_MODEL)),
                  _resident((D_MODEL, D_FF)), _resident((D_MODEL, D_FF)),
                  _resident((D_FF, D_MODEL)), _resident((1, D_MODEL)),
                  _resident((D_MODEL, QKVF_COLS)), _resident((1, LANES))],
        out_specs=[tile_a(D_MODEL), tile_a(2 * ATTN_W),
                   pl.BlockSpec((None, N_HEADS * VT_ROWS, TM_A), lambda b, i: (b, 0, i)),
                   tile_a(LANES)],
        out_shape=[jax.ShapeDtypeStruct((bsz, seq, D_MODEL), F32),
                   jax.ShapeDtypeStruct((bsz, seq, 2 * ATTN_W), BF16),
                   jax.ShapeDtypeStruct((bsz, N_HEADS * VT_ROWS, seq), BF16),
                   jax.ShapeDtypeStruct((bsz, seq, LANES), F32)],
        scratch_shapes=[pltpu.VMEM((TM_A, D_MODEL), BF16), pltpu.VMEM((TM_A, D_FF), BF16),
                        pltpu.VMEM((1, LANES), F32)],
        compiler_params=params,
        name="ffn1_qkv",
    )(x, row(ffn1_norm), ffn1_gate.astype(BF16), ffn1_up.astype(BF16),
      ffn1_down.astype(BF16), row(mix_norm), w_a, b_f)

    n_blk = seq // TQ
    y = pl.pallas_call(
        _fox_attn_body,
        grid=(bsz,),
        in_specs=[pl.BlockSpec((None, seq, 2 * ATTN_W), lambda b: (b, 0, 0)),
                  pl.BlockSpec((None, N_HEADS * VT_ROWS, seq), lambda b: (b, 0, 0)),
                  pl.BlockSpec((None, seq, LANES), lambda b: (b, 0, 0)),
                  _resident((N_SPLIT * LANES, ATTN_W)), _resident((N_SPLIT * LANES, ATTN_W))],
        out_specs=pl.BlockSpec((None, seq, ATTN_W), lambda b: (b, 0, 0)),
        out_shape=jax.ShapeDtypeStruct((bsz, seq, ATTN_W), BF16),
        scratch_shapes=[pltpu.VMEM((N_HEADS // 2, n_blk, 2 * TK, 2 * LANES), BF16),
                        pltpu.VMEM((n_blk, N_HEADS // 2, TQ, 2 * LANES), BF16),
                        pltpu.VMEM((2, N_HEADS, TQ), F32),
                        pltpu.VMEM((2, N_HEADS * VT_ROWS, TQ), F32),
                        pltpu.VMEM((2, ATTN_W, TQ), F32)],
        compiler_params=pltpu.CompilerParams(dimension_semantics=("arbitrary",),
                                             vmem_limit_bytes=VMEM_LIMIT),
        name="fox_attn",
    )(qk, vt, cum, sel_q, sel_k)

    tile_c = lambda w: pl.BlockSpec((None, TM_C, w), lambda b, i: (b, i, 0))
    return pl.pallas_call(
        _mix_ffn2_body,
        grid=(bsz, seq // TM_C),
        in_specs=[tile_c(D_MODEL), tile_c(ATTN_W), _resident((1, D_MODEL)),
                  _resident((D_MODEL, GATE_COLS)), _resident((3, CONV_W)),
                  _resident((ATTN_W, D_MODEL)), _resident((CONV_W, D_MODEL)),
                  _resident((D_MODEL, D_MODEL)), _resident((1, D_MODEL)),
                  _resident((D_MODEL, D_FF)), _resident((D_MODEL, D_FF)),
                  _resident((D_FF, D_MODEL)), _resident((1, D_MODEL))],
        out_specs=tile_c(D_MODEL),
        out_shape=jax.ShapeDtypeStruct((bsz, seq, D_MODEL), F32),
        scratch_shapes=[pltpu.VMEM((TM_C, D_MODEL), BF16), pltpu.VMEM((TM_C, D_FF), BF16),
                        pltpu.VMEM((2, CONV_W), F32)],
        compiler_params=params,
        name="mix_ffn2",
    )(x1, y, row(mix_norm), w_gate, conv_w.astype(F32), w_o_attn.astype(BF16),
      w_o_conv.astype(BF16), w_out.astype(BF16), row(ffn2_norm), ffn2_gate.astype(BF16),
      ffn2_up.astype(BF16), ffn2_down.astype(BF16), row(final_norm))
```
